```python
import jax, jax.numpy as jnp
from jax import lax
import numpy as np

D_MODEL = 2048
BATCH = 2
SEQ = 4096
DEPTH = 2
DEC_BATCH = 8
DEC_SEQ = 4
PAST_LEN = 16384
PAGE_SIZE = 128

N_A_LAYERS = DEPTH // 2
N_B_LAYERS = DEPTH - N_A_LAYERS
RET_HEADS = 8
RET_DK = D_MODEL // RET_HEADS
RET_DV = 2 * D_MODEL // RET_HEADS
RET_CHUNK = 128
ROPE_BASE = 10000.0
SB_HEADS = 16
SB_HD = D_MODEL // SB_HEADS
SB_BIAS_INIT = -10.0
Q_BLOCK = 128
D_FF = 4 * D_MODEL
NORM_EPS = 1e-6
GN_EPS = 1e-6

kernel_name = 'yoco_retention_stickbreaking_step'


def rmsnorm(x, g):
    x32 = x.astype(jnp.float32)
    y = x32 * lax.rsqrt(jnp.mean(x32 * x32, axis=-1, keepdims=True) + NORM_EPS)
    return (y * g.astype(jnp.float32)).astype(x.dtype)


def modulate(h, shift, scale):
    return h * (1 + scale[:, None, :]) + shift[:, None, :]


def rotary(x, pos):
    half = x.shape[-1] // 2
    freqs = ROPE_BASE ** (-jnp.arange(half, dtype=jnp.float32) / half)
    ang = pos.astype(jnp.float32)[:, None] * freqs[None, :]
    cos = jnp.cos(ang)[None, :, None, :]
    sin = jnp.sin(ang)[None, :, None, :]
    x32 = x.astype(jnp.float32)
    x1, x2 = x32[..., :half], x32[..., half:]
    return jnp.concatenate([x1 * cos - x2 * sin, x1 * sin + x2 * cos], axis=-1).astype(x.dtype)


def retention_log_gamma():
    return jnp.log1p(-jnp.exp2(-5.0 - jnp.arange(RET_HEADS, dtype=jnp.float32)))


def retention_chunk(q, k, v, S, log_g):
    L = q.shape[1]
    idx = jnp.arange(L, dtype=jnp.float32)
    diff = idx[:, None] - idx[None, :]
    causal = diff >= 0
    decay = jnp.where(causal[None], jnp.exp(jnp.where(causal, diff, 0.0)[None] * log_g[:, None, None]), 0.0)
    scores = jnp.einsum('blhd,bmhd->bhlm', q, k).astype(jnp.float32) * decay[None]
    v32 = v.astype(jnp.float32)
    intra = jnp.einsum('bhlm,bmhe->blhe', scores, v32)
    q_dec = q.astype(jnp.float32) * jnp.exp((idx[:, None] + 1.0) * log_g[None, :])[None, :, :, None]
    inter = jnp.einsum('blhd,bhde->blhe', q_dec, S)
    k_dec = k.astype(jnp.float32) * jnp.exp((L - 1.0 - idx)[:, None] * log_g[None, :])[None, :, :, None]
    S_new = jnp.exp(L * log_g)[None, :, None, None] * S + jnp.einsum('blhd,blhe->bhde', k_dec, v32)
    return intra + inter, S_new


def retention_mixer(h, pos, S0, w_in, w_out, gn_g):
    B, T, _ = h.shape
    proj = h @ w_in
    q, k, v, g = jnp.split(proj, [D_MODEL, 2 * D_MODEL, 4 * D_MODEL], axis=-1)
    q = rotary(q.reshape(B, T, RET_HEADS, RET_DK), pos)
    k = rotary(k.reshape(B, T, RET_HEADS, RET_DK), pos) * (RET_DK ** -0.5)
    v = v.reshape(B, T, RET_HEADS, RET_DV)
    log_g = retention_log_gamma()
    L = RET_CHUNK if T % RET_CHUNK == 0 else T
    nc = T // L

    def to_chunks(a):
        return a.reshape(B, nc, L, *a.shape[2:]).swapaxes(0, 1)

    def step(S, xs):
        qc, kc, vc = xs
        o, S = retention_chunk(qc, kc, vc, S, log_g)
        return S, o

    S_fin, o = lax.scan(step, S0.astype(jnp.float32), (to_chunks(q), to_chunks(k), to_chunks(v)))
    o = o.swapaxes(0, 1).reshape(B, T, RET_HEADS, RET_DV)
    mu = jnp.mean(o, axis=-1, keepdims=True)
    var = jnp.mean(jnp.square(o - mu), axis=-1, keepdims=True)
    o = ((o - mu) * lax.rsqrt(var + GN_EPS)).reshape(B, T, 2 * D_MODEL) * gn_g.astype(jnp.float32)
    y = (jax.nn.silu(g.astype(jnp.float32)) * o).astype(h.dtype) @ w_out
    return y, S_fin


def stick_breaking(q, k, v, bias, q_pos, k_pos):
    z = jnp.einsum('bqhd,bkhd->bhqk', q, k).astype(jnp.float32) * (SB_HD ** -0.5) \
        + bias.astype(jnp.float32)[None, :, None, None]
    mask = (k_pos[None, :] < q_pos[:, None])[None, None]
    log_keep = jnp.where(mask, jax.nn.log_sigmoid(-z), 0.0)
    suffix = lax.cumsum(log_keep, axis=3, reverse=True) - log_keep
    w = jnp.where(mask, jnp.exp(jax.nn.log_sigmoid(z) + suffix), 0.0)
    return jnp.einsum('bhqk,bkhd->bqhd', w.astype(v.dtype), v)


def stick_breaking_blocked(q, k, v, bias, q_pos, k_pos):
    B, Q, H, Dh = q.shape
    qb = Q_BLOCK if Q % Q_BLOCK == 0 else Q
    nb = Q // qb
    qs = q.reshape(B, nb, qb, H, Dh).swapaxes(0, 1)
    ps = q_pos.reshape(nb, qb)
    o = lax.map(lambda a: stick_breaking(a[0], k, v, bias, a[1], k_pos), (qs, ps))
    return o.swapaxes(0, 1).reshape(B, Q, H, Dh)


def trunk(x, c, pos, ret_s0, past_k, past_v, w_ada, b_ada, g_pre_mix, g_post_mix, g_pre_ffn, g_post_ffn,
          w_ff1, w_ff2, a_w_in, a_w_out, a_gn_g, kv_norm_g, w_kv, b_w_q, b_w_out, b_sb):
    B, T, _ = x.shape
    mods = jnp.einsum('bd,lde->lbe', jax.nn.silu(c), w_ada) + b_ada[:, None, :]
    new_ret = []
    k_new = v_new = k_all = v_all = k_pos = None
    for layer in range(DEPTH):
        sh1, sc1, gt1, sh2, sc2, gt2 = jnp.split(mods[layer], 6, axis=-1)
        h = modulate(rmsnorm(x, g_pre_mix[layer]), sh1, sc1)
        if layer < N_A_LAYERS:
            o, s = retention_mixer(h, pos, ret_s0[layer], a_w_in[layer], a_w_out[layer], a_gn_g[layer])
            new_ret.append(s)
        else:
            if k_new is None:
                kv = rmsnorm(x, kv_norm_g) @ w_kv
                k_new = kv[..., :D_MODEL].reshape(B, T, SB_HEADS, SB_HD)
                v_new = kv[..., D_MODEL:].reshape(B, T, SB_HEADS, SB_HD)
                if past_k is None:
                    k_all, v_all, k_pos = k_new, v_new, pos
                else:
                    k_all = jnp.concatenate([past_k.astype(k_new.dtype), k_new], axis=1)
                    v_all = jnp.concatenate([past_v.astype(v_new.dtype), v_new], axis=1)
                    k_pos = jnp.arange(past_k.shape[1] + T, dtype=jnp.int32)
            j = layer - N_A_LAYERS
            q = (h @ b_w_q[j]).reshape(B, T, SB_HEADS, SB_HD)
            o = stick_breaking_blocked(q, k_all, v_all, b_sb[j], pos, k_pos).reshape(B, T, D_MODEL) @ b_w_out[j]
        x = x + gt1[:, None, :] * rmsnorm(o, g_post_mix[layer])
        h = modulate(rmsnorm(x, g_pre_ffn[layer]), sh2, sc2)
        f = jnp.square(jax.nn.relu(h @ w_ff1[layer])) @ w_ff2[layer]
        x = x + gt2[:, None, :] * rmsnorm(f, g_post_ffn[layer])
    return x, jnp.stack(new_ret, axis=0), k_new, v_new


def setup_inputs(seed: int = 0) -> dict:
    key = jax.random.key(seed)
    ks = jax.random.split(key, 32)
    n_pages = PAST_LEN // PAGE_SIZE
    n_used = DEC_BATCH * n_pages
    n_pool = n_used + max(1, n_used // 4)
    f32 = jnp.float32

    def nrm(k, shape, scale):
        return jax.random.normal(k, shape, f32) * scale

    def gain(k, shape):
        return 1.0 + 0.02 * jax.random.normal(k, shape, f32)

    page_table = jax.random.permutation(ks[7], n_pool)[:n_used].reshape(DEC_BATCH, n_pages).astype(jnp.int32)
    return {
        'x_prompt': nrm(ks[0], (BATCH, SEQ, D_MODEL), 1.0),
        'x_sample': nrm(ks[1], (DEC_BATCH, DEC_SEQ, D_MODEL), 1.0),
        'c_prompt': nrm(ks[2], (BATCH, D_MODEL), 1.0),
        'c_sample': nrm(ks[3], (DEC_BATCH, D_MODEL), 1.0),
        'state_ret': nrm(ks[4], (N_A_LAYERS, DEC_BATCH, RET_HEADS, RET_DK, RET_DV), 0.3),
        'cache_k': nrm(ks[5], (n_pool, PAGE_SIZE, SB_HEADS, SB_HD), 1.0),
        'cache_v': nrm(ks[6], (n_pool, PAGE_SIZE, SB_HEADS, SB_HD), 1.0),
        'page_table': page_table,
        'w_ada': nrm(ks[8], (DEPTH, D_MODEL, 6 * D_MODEL), 0.5 * D_MODEL ** -0.5),
        'b_ada': nrm(ks[9], (DEPTH, 6 * D_MODEL), 0.02),
        'g_pre_mix': gain(ks[10], (DEPTH, D_MODEL)),
        'g_post_mix': gain(ks[11], (DEPTH, D_MODEL)),
        'g_pre_ffn': gain(ks[12], (DEPTH, D_MODEL)),
        'g_post_ffn': gain(ks[13], (DEPTH, D_MODEL)),
        'w_ff1': nrm(ks[14], (DEPTH, D_MODEL, D_FF), D_MODEL ** -0.5),
        'w_ff2': nrm(ks[15], (DEPTH, D_FF, D_MODEL), D_FF ** -0.5),
        'a_w_in': nrm(ks[16], (N_A_LAYERS, D_MODEL, 6 * D_MODEL), D_MODEL ** -0.5),
        'a_w_out': nrm(ks[17], (N_A_LAYERS, 2 * D_MODEL, D_MODEL), (2 * D_MODEL) ** -0.5),
        'a_gn_g': gain(ks[18], (N_A_LAYERS, 2 * D_MODEL)),
        'kv_norm_g': gain(ks[19], (D_MODEL,)),
        'w_kv': nrm(ks[20], (D_MODEL, 2 * D_MODEL), D_MODEL ** -0.5),
        'b_w_q': nrm(ks[21], (N_B_LAYERS, D_MODEL, D_MODEL), D_MODEL ** -0.5),
        'b_w_out': nrm(ks[22], (N_B_LAYERS, D_MODEL, D_MODEL), D_MODEL ** -0.5),
        'b_sb': SB_BIAS_INIT + nrm(ks[23], (N_B_LAYERS, SB_HEADS), 0.1),
    }


def reference(x_prompt, x_sample, c_prompt, c_sample, state_ret, cache_k, cache_v, page_table,
              w_ada, b_ada, g_pre_mix, g_post_mix, g_pre_ffn, g_post_ffn, w_ff1, w_ff2,
              a_w_in, a_w_out, a_gn_g, kv_norm_g, w_kv, b_w_q, b_w_out, b_sb):
    n_pages = PAST_LEN // PAGE_SIZE
    past_len = n_pages * PAGE_SIZE
    pos_p = jnp.arange(SEQ, dtype=jnp.int32)
    s0_p = jnp.zeros((N_A_LAYERS, BATCH, RET_HEADS, RET_DK, RET_DV), jnp.float32)
    y_prompt, state_ret_prompt, k_prompt, v_prompt = trunk(
        x_prompt, c_prompt, pos_p, s0_p, None, None, w_ada, b_ada, g_pre_mix, g_post_mix, g_pre_ffn,
        g_post_ffn, w_ff1, w_ff2, a_w_in, a_w_out, a_gn_g, kv_norm_g, w_kv, b_w_q, b_w_out, b_sb)
    past_k = cache_k[page_table].reshape(DEC_BATCH, past_len, SB_HEADS, SB_HD)
    past_v = cache_v[page_table].reshape(DEC_BATCH, past_len, SB_HEADS, SB_HD)
    pos_s = past_len + jnp.arange(DEC_SEQ, dtype=jnp.int32)
    y_sample, state_ret_sample, k_sample, v_sample = trunk(
        x_sample, c_sample, pos_s, state_ret, past_k, past_v, w_ada, b_ada, g_pre_mix, g_post_mix, g_pre_ffn,
        g_post_ffn, w_ff1, w_ff2, a_w_in, a_w_out, a_gn_g, kv_norm_g, w_kv, b_w_q, b_w_out, b_sb)
    return (y_prompt, y_sample, state_ret_prompt, state_ret_sample, k_prompt, v_prompt, k_sample, v_sample)
```

```python
import functools

import jax
import jax.numpy as jnp
import numpy as np
from jax import lax
from jax.experimental import pallas as pl
from jax.experimental.pallas import tpu as pltpu

F32 = jnp.float32
BF16 = jnp.bfloat16

LANES = 128
SUBLANES = 8
VMEM_LIMIT_BYTES = 56 * 1024 * 1024

RET_HEADS = 8
RET_CHUNK = 128
ROPE_BASE = 10000.0
SB_HEADS = 16
PAGE_SIZE = 128
NORM_EPS = 1e-6
GN_EPS = 1e-6
SAMPLE_ROWS = SUBLANES


def _params(*sem):
    return pltpu.CompilerParams(dimension_semantics=sem, vmem_limit_bytes=VMEM_LIMIT_BYTES)


def _mod_spec(mod, tile_rows, rows_per_group, row_axis):
    d = mod.shape[-1]
    if mod.shape[1] == 1:
        return pl.BlockSpec((1, 1, d), lambda *g: ((g[row_axis] * tile_rows) // rows_per_group, 0, 0))
    return pl.BlockSpec((1, tile_rows, d), lambda *g: (0, g[row_axis], 0))


def _ada_kernel(c_ref, w_ref, b_ref, o_ref):
    c = c_ref[...]
    a = (c * jax.nn.sigmoid(c)).astype(BF16)
    o_ref[0] = jnp.dot(a, w_ref[0].astype(BF16), preferred_element_type=F32) + b_ref[0]


def _ada(c, w_ada, b_ada, tn=1024):
    depth, d, n = w_ada.shape
    m = c.shape[0]
    return pl.pallas_call(
        _ada_kernel,
        grid=(depth, n // tn),
        in_specs=[pl.BlockSpec((m, d), lambda l, j: (0, 0)),
                  pl.BlockSpec((1, d, tn), lambda l, j: (l, 0, j)),
                  pl.BlockSpec((1, 1, tn), lambda l, j: (l, 0, j))],
        out_specs=pl.BlockSpec((1, m, tn), lambda l, j: (l, 0, j)),
        out_shape=jax.ShapeDtypeStruct((depth, m, n), F32),
        compiler_params=_params("arbitrary", "arbitrary"),
        name="ada_mods",
    )(c, w_ada, b_ada.reshape(depth, 1, n))


def _norm_kernel(*refs, modulated):
    if modulated:
        x_ref, g_ref, sc_ref, sh_ref, o_ref = refs
    else:
        x_ref, g_ref, o_ref = refs
    x = x_ref[...]
    y = x * lax.rsqrt(jnp.mean(x * x, axis=-1, keepdims=True) + NORM_EPS) * g_ref[0]
    if modulated:
        y = y * (1.0 + sc_ref[0]) + sh_ref[0]
    o_ref[...] = y.astype(o_ref.dtype)


def _norm(x, g3, layer, sc=None, sh=None, groups=1, tr=512):
    m, d = x.shape
    tr = min(tr, m)
    rpg = m // groups
    modulated = sc is not None
    in_specs = [pl.BlockSpec((tr, d), lambda i: (i, 0)),
                pl.BlockSpec((1, 1, d), lambda i: (layer, 0, 0))]
    args = [x, g3]
    if modulated:
        in_specs += [_mod_spec(sc, tr, rpg, 0), _mod_spec(sh, tr, rpg, 0)]
        args += [sc, sh]
    return pl.pallas_call(
        functools.partial(_norm_kernel, modulated=modulated),
        grid=(m // tr,),
        in_specs=in_specs,
        out_specs=pl.BlockSpec((tr, d), lambda i: (i, 0)),
        out_shape=jax.ShapeDtypeStruct((m, d), BF16),
        compiler_params=_params("arbitrary"),
        name="norm_mod",
    )(*args)


def _mm_kernel(*refs, epilogue, nk):
    if epilogue == "resid":
        a_ref, w_ref, x_ref, gt_ref, g_ref, o_ref = refs[:6]
        rest = refs[6:]
    else:
        a_ref, w_ref, o_ref = refs[:3]
        rest = refs[3:]

    def finish(acc):
        if epilogue == "relu2":
            r = jnp.maximum(acc, 0.0)
            acc = r * r
        elif epilogue == "resid":
            n = acc * lax.rsqrt(jnp.mean(acc * acc, axis=-1, keepdims=True) + NORM_EPS) * g_ref[0]
            acc = x_ref[...] + gt_ref[0] * n
        o_ref[...] = acc.astype(o_ref.dtype)

    prod = jnp.dot(a_ref[...], w_ref[0].astype(BF16), preferred_element_type=F32)
    if nk == 1:
        finish(prod)
    else:
        acc_ref, = rest
        k = pl.program_id(2)

        @pl.when(k == 0)
        def _():
            acc_ref[...] = prod

        @pl.when(k > 0)
        def _():
            acc_ref[...] += prod

        @pl.when(k == nk - 1)
        def _():
            finish(acc_ref[...])


def _mm(a, w3, layer, *, out_dtype, n_out=None, col0=0, epilogue="none", resid=None, groups=1,
        tm=1024, tn=1024, tk=1024):
    m, kdim = a.shape
    n = w3.shape[2] if n_out is None else n_out
    tm, tn, tk = min(tm, m), min(tn, n), min(tk, kdim)
    if epilogue == "resid":
        tm, tn, tk = min(tm, 512), n, min(tk, 512)
    nk = kdim // tk
    cb = col0 // tn
    in_specs = [pl.BlockSpec((tm, tk), lambda i, j, k: (i, k)),
                pl.BlockSpec((1, tk, tn), lambda i, j, k: (layer, k, j + cb))]
    args = [a, w3]
    if epilogue == "resid":
        x, gt, g3, glayer = resid
        in_specs += [pl.BlockSpec((tm, n), lambda i, j, k: (i, 0)),
                     _mod_spec(gt, tm, m // groups, 0),
                     pl.BlockSpec((1, 1, n), lambda i, j, k: (glayer, 0, 0))]
        args += [x, gt, g3]
    return pl.pallas_call(
        functools.partial(_mm_kernel, epilogue=epilogue, nk=nk),
        grid=(m // tm, n // tn, nk),
        in_specs=in_specs,
        out_specs=pl.BlockSpec((tm, tn), lambda i, j, k: (i, j)),
        out_shape=jax.ShapeDtypeStruct((m, n), out_dtype),
        scratch_shapes=[pltpu.VMEM((tm, tn), F32)] if nk > 1 else [],
        compiler_params=_params("arbitrary", "arbitrary", "arbitrary"),
        name="mm_" + epilogue,
    )(*args)


def _ret_kernel(lg_ref, q_ref, k_ref, v_ref, g_ref, cos_ref, sin_ref, gng_ref, s0_ref,
                y_ref, sfin_ref, s_scr, *, chunk, n_sub, valid, dk, mxu_dtype):
    h = pl.program_id(1)
    c = pl.program_id(2)

    @pl.when(c == 0)
    def _():
        s_scr[...] = s0_ref[0, 0]

    lg = lg_ref[h]
    row = lax.broadcasted_iota(jnp.int32, (chunk, 1), 0).astype(F32)
    col = lax.broadcasted_iota(jnp.int32, (1, chunk), 1).astype(F32)
    diff = row - col
    causal = diff >= 0.0
    decay = jnp.where(causal, jnp.exp(jnp.where(causal, diff, 0.0) * lg), 0.0)
    q_scale = jnp.exp((row + 1.0) * lg)
    k_scale = jnp.where(row < valid, jnp.exp((valid - 1.0 - row) * lg), 0.0)
    s_scale = jnp.exp(jnp.full((1, s_scr.shape[1]), valid, F32) * lg)
    half = dk // 2

    def rot(x, cos, sin):
        x1, x2 = x[:, :half], x[:, half:]
        return jnp.concatenate([x1 * cos - x2 * sin, x1 * sin + x2 * cos], axis=1)

    for sub in range(n_sub):
        rows = pl.ds(sub * chunk, chunk)
        cos, sin = cos_ref[rows, :], sin_ref[rows, :]
        q = rot(q_ref[0, rows, :], cos, sin)
        k = rot(k_ref[0, rows, :], cos, sin) * (dk ** -0.5)
        v = v_ref[0, rows, :].astype(mxu_dtype)
        s = s_scr[...]
        scores = lax.dot_general(q.astype(mxu_dtype), k.astype(mxu_dtype), (((1,), (1,)), ((), ())),
                                 preferred_element_type=F32) * decay
        o = jnp.dot(scores.astype(mxu_dtype), v, preferred_element_type=F32)
        o = o + jnp.dot((q * q_scale).astype(mxu_dtype), s.astype(mxu_dtype), preferred_element_type=F32)
        kd = (k * k_scale).astype(mxu_dtype)
        s_scr[...] = s_scale * s + lax.dot_general(kd, v, (((0,), (0,)), ((), ())),
                                                   preferred_element_type=F32)
        mu = jnp.mean(o, axis=-1, keepdims=True)
        ctr = o - mu
        var = jnp.mean(ctr * ctr, axis=-1, keepdims=True)
        on = ctr * lax.rsqrt(var + GN_EPS) * gng_ref[0]
        g = g_ref[0, rows, :]
        y_ref[0, rows, :] = (g * jax.nn.sigmoid(g) * on).astype(y_ref.dtype)

    @pl.when(c == pl.num_programs(2) - 1)
    def _():
        sfin_ref[0, 0] = s_scr[...]


def _retention(proj, cos, sin, s0, gn_g3, log_g, *, batch, chunk, valid, n_sub):
    m, six_d = proj.shape
    d = six_d // 6
    t = m // batch
    dk, dv = d // RET_HEADS, 2 * d // RET_HEADS
    rows = chunk * n_sub
    proj3 = proj.reshape(batch, t, six_d)
    mxu_dtype = BF16 if chunk % 16 == 0 else F32
    kern = functools.partial(_ret_kernel, chunk=chunk, n_sub=n_sub, valid=float(valid), dk=dk,
                             mxu_dtype=mxu_dtype)
    y, s_fin = pl.pallas_call(
        kern,
        grid=(batch, RET_HEADS, t // rows),
        in_specs=[pl.BlockSpec(memory_space=pltpu.SMEM),
                  pl.BlockSpec((1, rows, dk), lambda b, h, c: (b, c, h)),
                  pl.BlockSpec((1, rows, dk), lambda b, h, c: (b, c, RET_HEADS + h)),
                  pl.BlockSpec((1, rows, dv), lambda b, h, c: (b, c, RET_HEADS + h)),
                  pl.BlockSpec((1, rows, dv), lambda b, h, c: (b, c, 2 * RET_HEADS + h)),
                  pl.BlockSpec((rows, dk // 2), lambda b, h, c: (c, 0)),
                  pl.BlockSpec((rows, dk // 2), lambda b, h, c: (c, 0)),
                  pl.BlockSpec((1, 1, dv), lambda b, h, c: (0, 0, h)),
                  pl.BlockSpec((1, 1, dk, dv), lambda b, h, c: (b, h, 0, 0))],
        out_specs=[pl.BlockSpec((1, rows, dv), lambda b, h, c: (b, c, h)),
                   pl.BlockSpec((1, 1, dk, dv), lambda b, h, c: (b, h, 0, 0))],
        out_shape=[jax.ShapeDtypeStruct((batch, t, 2 * d), mxu_dtype),
                   jax.ShapeDtypeStruct((batch, RET_HEADS, dk, dv), F32)],
        scratch_shapes=[pltpu.VMEM((dk, dv), F32)],
        compiler_params=_params("arbitrary", "arbitrary", "arbitrary"),
        name="retention",
    )(log_g, proj3, proj3, proj3, proj3, cos, sin, gn_g3, s0)
    return y.reshape(m, 2 * d).astype(BF16), s_fin


def _suffix_matrix():
    j = np.arange(2 * LANES)[:, None] % LANES
    c = np.arange(2 * LANES)[None, :]
    return jnp.asarray(np.where(c < LANES, j > c, True), dtype=BF16)


def _sb_block(z, mask, carry, u_ref):
    t = jnp.log1p(jnp.exp(-jnp.abs(z)))
    log_beta = jnp.minimum(z, 0.0) - t
    log_keep = jnp.minimum(-z, 0.0) - t
    if mask is not None:
        log_keep = jnp.where(mask, log_keep, 0.0)
    hi = log_keep.astype(BF16)
    lo = (log_keep - hi.astype(F32)).astype(BF16)
    r = jnp.dot(jnp.concatenate([hi, lo], axis=1), u_ref[...], preferred_element_type=F32)
    w = jnp.exp(log_beta + r[:, :LANES] + carry)
    if mask is not None:
        w = jnp.where(mask, w, 0.0)
    return w, carry + r[:, LANES:]


def _sb_prompt_kernel(bias_ref, u_ref, q_ref, k_ref, v_ref, o_ref, o_scr, c_scr, *, tq, scale):
    h = pl.program_id(1)
    qi = pl.program_id(2)
    bias = bias_ref[h]
    q = q_ref[0]
    o_scr[...] = jnp.zeros_like(o_scr)
    c_scr[...] = jnp.zeros_like(c_scr)
    n_blocks = (qi + 1) * (tq // LANES)
    q_pos = qi * tq + lax.broadcasted_iota(jnp.int32, (tq, LANES), 0)
    k_off = lax.broadcasted_iota(jnp.int32, (tq, LANES), 1)

    def body(i, _):
        j = n_blocks - 1 - i
        rows = pl.ds(pl.multiple_of(j * LANES, LANES), LANES)
        ks = k_ref[0, rows, :].astype(BF16)
        vs = v_ref[0, rows, :].astype(BF16)
        z = lax.dot_general(q, ks, (((1,), (1,)), ((), ())), preferred_element_type=F32) * scale + bias
        mask = (j * LANES + k_off) < q_pos
        w, carry = _sb_block(z, mask, c_scr[...], u_ref)
        o_scr[...] += jnp.dot(w.astype(BF16), vs, preferred_element_type=F32)
        c_scr[...] = carry
        return 0

    lax.fori_loop(0, n_blocks, body, 0)
    o_ref[0] = o_scr[...].astype(o_ref.dtype)


def _sb_prompt(q, k, v, bias, *, batch, tq=256):
    m, d = q.shape
    t = m // batch
    hd = d // SB_HEADS
    q3, k3, v3 = (a.reshape(batch, t, d) for a in (q, k, v))
    o = pl.pallas_call(
        functools.partial(_sb_prompt_kernel, tq=tq, scale=hd ** -0.5),
        grid=(batch, SB_HEADS, t // tq),
        in_specs=[pl.BlockSpec(memory_space=pltpu.SMEM),
                  pl.BlockSpec((2 * LANES, 2 * LANES), lambda b, h, i: (0, 0)),
                  pl.BlockSpec((1, tq, hd), lambda b, h, i: (b, i, h)),
                  pl.BlockSpec((1, t, hd), lambda b, h, i: (b, 0, h)),
                  pl.BlockSpec((1, t, hd), lambda b, h, i: (b, 0, h))],
        out_specs=pl.BlockSpec((1, tq, hd), lambda b, h, i: (b, i, h)),
        out_shape=jax.ShapeDtypeStruct((batch, t, d), BF16),
        scratch_shapes=[pltpu.VMEM((tq, hd), F32), pltpu.VMEM((tq, LANES), F32)],
        compiler_params=_params("arbitrary", "arbitrary", "arbitrary"),
        name="sb_prompt",
    )(bias, _suffix_matrix(), q3, k3, v3)
    return o.reshape(m, d)


def _sb_sample_kernel(pt_ref, bias_ref, u_ref, q_ref, kn_ref, vn_ref, kc_ref, vc_ref, o_ref,
                      o_scr, c_scr, *, scale, n_q):
    s = pl.program_id(1)
    rq = SAMPLE_ROWS

    @pl.when(s == 0)
    def _():
        o_scr[...] = jnp.zeros_like(o_scr)
        c_scr[...] = jnp.zeros_like(c_scr)

    def step(k_ref, v_ref, masked):
        zs = []
        for h in range(SB_HEADS):
            kh = k_ref[0, :, h, :].astype(BF16)
            zh = lax.dot_general(q_ref[0, h].astype(BF16), kh, (((1,), (1,)), ((), ())),
                                 preferred_element_type=F32)
            zs.append(zh * scale + bias_ref[h])
        z = jnp.concatenate(zs, axis=0)
        if masked:
            t_q = lax.broadcasted_iota(jnp.int32, z.shape, 0) % rq
            mask = lax.broadcasted_iota(jnp.int32, z.shape, 1) < jnp.minimum(t_q, n_q)
        else:
            mask = None
        w, carry = _sb_block(z, mask, c_scr[...], u_ref)
        c_scr[...] = carry
        for h in range(SB_HEADS):
            vh = v_ref[0, :, h, :].astype(BF16)
            o_scr[h] += jnp.dot(w[h * rq:(h + 1) * rq, :].astype(BF16), vh, preferred_element_type=F32)

    @pl.when(s == 0)
    def _():
        step(kn_ref, vn_ref, True)

    @pl.when(s > 0)
    def _():
        step(kc_ref, vc_ref, False)

    @pl.when(s == pl.num_programs(1) - 1)
    def _():
        o_ref[0] = o_scr[...]


def _sb_sample(q, k_new_page, v_new_page, cache_k, cache_v, page_table, bias, *, n_q):
    b, n_pages = page_table.shape
    hd = q.shape[-1]

    def cache_map(i, s, pt):
        return (pt[i, n_pages - jnp.maximum(s, 1)], 0, 0, 0)

    grid_spec = pltpu.PrefetchScalarGridSpec(
        num_scalar_prefetch=1,
        grid=(b, n_pages + 1),
        in_specs=[pl.BlockSpec(memory_space=pltpu.SMEM),
                  pl.BlockSpec((2 * LANES, 2 * LANES), lambda i, s, pt: (0, 0)),
                  pl.BlockSpec((1, SB_HEADS, SAMPLE_ROWS, hd), lambda i, s, pt: (i, 0, 0, 0)),
                  pl.BlockSpec((1, PAGE_SIZE, SB_HEADS, hd), lambda i, s, pt: (i, 0, 0, 0)),
                  pl.BlockSpec((1, PAGE_SIZE, SB_HEADS, hd), lambda i, s, pt: (i, 0, 0, 0)),
                  pl.BlockSpec((1, PAGE_SIZE, SB_HEADS, hd), cache_map),
                  pl.BlockSpec((1, PAGE_SIZE, SB_HEADS, hd), cache_map)],
        out_specs=pl.BlockSpec((1, SB_HEADS, SAMPLE_ROWS, hd), lambda i, s, pt: (i, 0, 0, 0)),
        scratch_shapes=[pltpu.VMEM((SB_HEADS, SAMPLE_ROWS, hd), F32),
                        pltpu.VMEM((SB_HEADS * SAMPLE_ROWS, LANES), F32)],
    )
    return pl.pallas_call(
        functools.partial(_sb_sample_kernel, scale=hd ** -0.5, n_q=n_q),
        grid_spec=grid_spec,
        out_shape=jax.ShapeDtypeStruct((b, SB_HEADS, SAMPLE_ROWS, hd), F32),
        compiler_params=_params("arbitrary", "arbitrary"),
        name="sb_sample",
    )(page_table, bias, _suffix_matrix(), q, k_new_page, v_new_page, cache_k, cache_v)


def _rope_tables(pos, half):
    freqs = ROPE_BASE ** (-jnp.arange(half, dtype=F32) / half)
    ang = pos.astype(F32)[:, None] * freqs[None, :]
    return jnp.cos(ang), jnp.sin(ang)


def _trunk(x, mods, groups, weights, retention_fn, attention_fn, tiles):
    (g_pre_mix, g_post_mix, g_pre_ffn, g_post_ffn, w_ff1, w_ff2, a_w_in, a_w_out, kv_norm_g, w_kv,
     b_w_q, b_w_out) = weights
    d = x.shape[1]
    n_a = a_w_in.shape[0]
    depth = w_ff1.shape[0]
    states = []
    k_new = v_new = None
    for layer in range(depth):
        sh1, sc1, gt1, sh2, sc2, gt2 = mods[layer]
        h = _norm(x, g_pre_mix, layer, sc1, sh1, groups)
        if layer < n_a:
            proj = _mm(h, a_w_in, layer, out_dtype=F32, **tiles)
            y, s_fin = retention_fn(layer, proj)
            states.append(s_fin)
            x = _mm(y, a_w_out, layer, out_dtype=F32, epilogue="resid",
                    resid=(x, gt1, g_post_mix, layer), groups=groups, **tiles)
        else:
            if k_new is None:
                hn = _norm(x, kv_norm_g, 0)
                k_new = _mm(hn, w_kv, 0, out_dtype=F32, n_out=d, col0=0, **tiles)
                v_new = _mm(hn, w_kv, 0, out_dtype=F32, n_out=d, col0=d, **tiles)
            j = layer - n_a
            o = attention_fn(j, h, k_new, v_new)
            x = _mm(o, b_w_out, j, out_dtype=F32, epilogue="resid",
                    resid=(x, gt1, g_post_mix, layer), groups=groups, **tiles)
        h = _norm(x, g_pre_ffn, layer, sc2, sh2, groups)
        f = _mm(h, w_ff1, layer, out_dtype=BF16, epilogue="relu2", **tiles)
        x = _mm(f, w_ff2, layer, out_dtype=F32, epilogue="resid",
                resid=(x, gt2, g_post_ffn, layer), groups=groups, **tiles)
    return x, jnp.stack(states, axis=0), k_new, v_new


def kernel(x_prompt, x_sample, c_prompt, c_sample, state_ret, cache_k, cache_v, page_table, w_ada, b_ada,
           g_pre_mix, g_post_mix, g_pre_ffn, g_post_ffn, w_ff1, w_ff2, a_w_in, a_w_out, a_gn_g, kv_norm_g,
           w_kv, b_w_q, b_w_out, b_sb):
    bp, seq, d = x_prompt.shape
    bs, dec_seq, _ = x_sample.shape
    depth = w_ada.shape[0]
    n_pages = page_table.shape[1]
    past_len = n_pages * cache_k.shape[1]
    hd = d // SB_HEADS
    dk, dv = d // RET_HEADS, 2 * d // RET_HEADS
    rs = SAMPLE_ROWS

    weights = (g_pre_mix.reshape(depth, 1, d), g_post_mix.reshape(depth, 1, d),
               g_pre_ffn.reshape(depth, 1, d), g_post_ffn.reshape(depth, 1, d),
               w_ff1, w_ff2, a_w_in, a_w_out, kv_norm_g.reshape(1, 1, d), w_kv[None], b_w_q, b_w_out)
    gn_g3 = a_gn_g.reshape(a_gn_g.shape[0], 1, 2 * d)
    log_g = jnp.log1p(-jnp.exp2(-5.0 - jnp.arange(RET_HEADS, dtype=F32)))

    n_c = bp + bs
    c_rows = -(-n_c // SUBLANES) * SUBLANES
    c_all = jnp.concatenate([c_prompt, c_sample, jnp.zeros((c_rows - n_c, d), F32)], axis=0)
    mods_all = _ada(c_all, w_ada, b_ada)
    mods_p = [[m[:, None, :] for m in jnp.split(mods_all[l, :bp], 6, axis=-1)] for l in range(depth)]
    mods_s = [[jnp.repeat(m, rs, axis=0)[None] for m in jnp.split(mods_all[l, bp:n_c], 6, axis=-1)]
              for l in range(depth)]

    cos_p, sin_p = _rope_tables(jnp.arange(seq, dtype=jnp.int32), dk // 2)
    s0_p = jnp.zeros((bp, RET_HEADS, dk, dv), F32)

    def ret_prompt(layer, proj):
        return _retention(proj, cos_p, sin_p, s0_p, gn_g3[layer:layer + 1], log_g, batch=bp,
                          chunk=RET_CHUNK, valid=RET_CHUNK, n_sub=4)

    def attn_prompt(j, h, k_new, v_new):
        q = _mm(h, b_w_q, j, out_dtype=BF16)
        return _sb_prompt(q, k_new, v_new, b_sb[j], batch=bp)

    y_p, st_p, k_p, v_p = _trunk(x_prompt.reshape(bp * seq, d), mods_p, bp, weights, ret_prompt,
                                 attn_prompt, dict(tm=1024, tn=1024, tk=1024))

    x_s = jnp.pad(x_sample, ((0, 0), (0, rs - dec_seq), (0, 0))).reshape(bs * rs, d)
    cos_s, sin_s = _rope_tables(past_len + jnp.arange(rs, dtype=jnp.int32), dk // 2)

    def ret_sample(layer, proj):
        return _retention(proj, cos_s, sin_s, state_ret[layer], gn_g3[layer:layer + 1], log_g, batch=bs,
                          chunk=rs, valid=dec_seq, n_sub=1)

    def attn_sample(j, h, k_new, v_new):
        q = _mm(h, b_w_q, j, out_dtype=F32, tn=1024, tk=1024)
        q4 = q.reshape(bs, rs, SB_HEADS, hd).transpose(0, 2, 1, 3)
        pad = ((0, 0), (0, PAGE_SIZE - rs), (0, 0), (0, 0))
        kn = jnp.pad(k_new.reshape(bs, rs, SB_HEADS, hd), pad)
        vn = jnp.pad(v_new.reshape(bs, rs, SB_HEADS, hd), pad)
        o = _sb_sample(q4, kn, vn, cache_k, cache_v, page_table, b_sb[j], n_q=dec_seq)
        return o.transpose(0, 2, 1, 3).reshape(bs * rs, d).astype(BF16)

    y_s, st_s, k_s, v_s = _trunk(x_s, mods_s, 1, weights, ret_sample, attn_sample,
                                 dict(tm=bs * rs, tn=1024, tk=1024))

    def unpad(a, *tail):
        return a.reshape(bs, rs, *tail)[:, :dec_seq]

    return (y_p.reshape(bp, seq, d), unpad(y_s, d), st_p, st_s,
            k_p.reshape(bp, seq, SB_HEADS, hd), v_p.reshape(bp, seq, SB_HEADS, hd),
            unpad(k_s, SB_HEADS, hd), unpad(v_s, SB_HEADS, hd))
```

```python
import functools

import jax
import jax.numpy as jnp
import numpy as np
from jax import lax
from jax.experimental import pallas as pl
from jax.experimental.pallas import tpu as pltpu

F32 = jnp.float32
BF16 = jnp.bfloat16

LANES = 128
SUBLANES = 8
VMEM_LIMIT_BYTES = 56 * 1024 * 1024

RET_HEADS = 8
RET_CHUNK = 128
ROPE_BASE = 10000.0
SB_HEADS = 16
PAGE_SIZE = 128
NORM_EPS = 1e-6
GN_EPS = 1e-6
SAMPLE_ROWS = SUBLANES
NT_DIMS = (((1,), (1,)), ((), ()))


def _params(*sem):
    return pltpu.CompilerParams(dimension_semantics=sem, vmem_limit_bytes=VMEM_LIMIT_BYTES)


def _mod_spec(mod, tile_rows, rows_per_group):
    d = mod.shape[-1]
    if mod.shape[1] == 1:
        return pl.BlockSpec((1, 1, d), lambda *g: ((g[0] * tile_rows) // rows_per_group, 0, 0))
    return pl.BlockSpec((1, tile_rows, d), lambda *g: (0, g[0], 0))


def _rms(v, g):
    return v * lax.rsqrt(jnp.mean(v * v, axis=-1, keepdims=True) + NORM_EPS) * g


def _ada_kernel(c_ref, w_ref, b_ref, o_ref):
    c = c_ref[...]
    a = (c * jax.nn.sigmoid(c)).astype(BF16)
    o_ref[0] = jnp.dot(a, w_ref[0].astype(BF16), preferred_element_type=F32) + b_ref[0]


def _ada(c, w_ada, b_ada, tn=1024):
    depth, d, n = w_ada.shape
    m = c.shape[0]
    return pl.pallas_call(
        _ada_kernel,
        grid=(depth, n // tn),
        in_specs=[pl.BlockSpec((m, d), lambda l, j: (0, 0)),
                  pl.BlockSpec((1, d, tn), lambda l, j: (l, 0, j)),
                  pl.BlockSpec((1, 1, tn), lambda l, j: (l, 0, j))],
        out_specs=pl.BlockSpec((1, m, tn), lambda l, j: (l, 0, j)),
        out_shape=jax.ShapeDtypeStruct((depth, m, n), F32),
        compiler_params=_params("arbitrary", "arbitrary"),
        name="ada_mods",
    )(c, w_ada, b_ada.reshape(depth, 1, n))


def _norm_kernel(x_ref, g_ref, sc_ref, sh_ref, o_ref):
    y = _rms(x_ref[...], g_ref[0]) * (1.0 + sc_ref[0]) + sh_ref[0]
    o_ref[...] = y.astype(o_ref.dtype)


def _norm(x, g3, layer, sc, sh, groups, tr=512):
    m, d = x.shape
    tr = min(tr, m)
    return pl.pallas_call(
        _norm_kernel,
        grid=(m // tr,),
        in_specs=[pl.BlockSpec((tr, d), lambda i: (i, 0)),
                  pl.BlockSpec((1, 1, d), lambda i: (layer, 0, 0)),
                  _mod_spec(sc, tr, m // groups), _mod_spec(sh, tr, m // groups)],
        out_specs=pl.BlockSpec((tr, d), lambda i: (i, 0)),
        out_shape=jax.ShapeDtypeStruct((m, d), BF16),
        compiler_params=_params("arbitrary"),
        name="norm_mod",
    )(x, g3, sc, sh)


def _mm_kernel(*refs, epilogue, nk, nexts, scale):
    a_ref, w_ref = refs[:2]
    pos = 2
    next_in = []
    if epilogue == "resid":
        x_ref, gt_ref, g_ref = refs[2:5]
        pos = 5
        for modulated in nexts:
            n_in = 3 if modulated else 1
            next_in.append(refs[pos:pos + n_in])
            pos += n_in
    o_ref = refs[pos]
    next_out = refs[pos + 1:pos + 1 + len(nexts)]
    acc_ref = refs[pos + 1 + len(nexts)] if nk > 1 else None

    def prod():
        return jnp.dot(a_ref[...], w_ref[0], preferred_element_type=F32)

    def finish(acc):
        if epilogue == "relu2":
            r = jnp.maximum(acc, 0.0)
            acc = r * r
        elif epilogue == "scale":
            acc = acc * scale
        elif epilogue == "resid":
            acc = x_ref[...] + gt_ref[0] * _rms(acc, g_ref[0])
            for ins, out in zip(next_in, next_out):
                y = _rms(acc, ins[0][0])
                if len(ins) == 3:
                    y = y * (1.0 + ins[1][0]) + ins[2][0]
                out[...] = y.astype(out.dtype)
        o_ref[...] = acc.astype(o_ref.dtype)

    if nk == 1:
        finish(prod())
        return
    k = pl.program_id(2)

    @pl.when(k == 0)
    def _():
        acc_ref[...] = prod()

    @pl.when(jnp.logical_and(k > 0, k < nk - 1))
    def _():
        acc_ref[...] += prod()

    @pl.when(k == nk - 1)
    def _():
        finish(acc_ref[...] + prod())


def _mm(a, w3, layer, *, out_dtype, n_out=None, col0=0, epilogue="none", scale=1.0, resid=None, nexts=(),
        groups=1, tm=1024, tn=1024, tk=2048):
    m, kdim = a.shape
    n = w3.shape[2] if n_out is None else n_out
    tm, tn, tk = min(tm, m), min(tn, n), min(tk, kdim)
    if epilogue == "resid":
        tm, tn, tk = min(tm, 512), n, min(tk, 1024)
    nk = kdim // tk
    cb = col0 // tn
    rpg = m // groups
    in_specs = [pl.BlockSpec((tm, tk), lambda i, j, k: (i, k)),
                pl.BlockSpec((1, tk, tn), lambda i, j, k: (layer, k, j + cb))]
    args = [a, w3]
    out_specs = [pl.BlockSpec((tm, tn), lambda i, j, k: (i, j))]
    out_shape = [jax.ShapeDtypeStruct((m, n), out_dtype)]
    if epilogue == "resid":
        x, gt, g3, glayer = resid
        in_specs += [pl.BlockSpec((tm, n), lambda i, j, k: (i, 0)), _mod_spec(gt, tm, rpg),
                     pl.BlockSpec((1, 1, n), lambda i, j, k: (glayer, 0, 0))]
        args += [x, gt, g3]
        for ng3, nlayer, sc, sh in nexts:
            in_specs.append(pl.BlockSpec((1, 1, n), lambda i, j, k, nlayer=nlayer: (nlayer, 0, 0)))
            args.append(ng3)
            if sc is not None:
                in_specs += [_mod_spec(sc, tm, rpg), _mod_spec(sh, tm, rpg)]
                args += [sc, sh]
            out_specs.append(pl.BlockSpec((tm, n), lambda i, j, k: (i, 0)))
            out_shape.append(jax.ShapeDtypeStruct((m, n), BF16))
    outs = pl.pallas_call(
        functools.partial(_mm_kernel, epilogue=epilogue, nk=nk, scale=scale,
                          nexts=tuple(sc is not None for _, _, sc, _ in nexts)),
        grid=(m // tm, n // tn, nk),
        in_specs=in_specs,
        out_specs=out_specs,
        out_shape=out_shape,
        scratch_shapes=[pltpu.VMEM((tm, tn), F32)] if nk > 1 else [],
        compiler_params=_params("arbitrary", "arbitrary", "arbitrary"),
        name="mm_" + epilogue,
    )(*args)
    return outs if epilogue == "resid" else outs[0]


def _ret_kernel(lg_ref, q_ref, k_ref, v_ref, g_ref, cos_ref, sin_ref, gng_ref, s0_ref,
                y_ref, sfin_ref, s_scr, *, chunk, n_sub, valid, dk, mxu_dtype):
    h = pl.program_id(1)
    c = pl.program_id(2)

    @pl.when(c == 0)
    def _():
        s_scr[...] = s0_ref[0, 0]

    lg = lg_ref[h]
    row = lax.broadcasted_iota(jnp.int32, (chunk, 1), 0).astype(F32)
    col = lax.broadcasted_iota(jnp.int32, (1, chunk), 1).astype(F32)
    diff = row - col
    causal = diff >= 0.0
    decay = jnp.where(causal, jnp.exp(jnp.where(causal, diff, 0.0) * lg), 0.0)
    q_scale = jnp.exp((row + 1.0) * lg)
    k_scale = jnp.where(row < valid, jnp.exp((valid - 1.0 - row) * lg), 0.0)
    s_scale = jnp.exp(jnp.full((1, s_scr.shape[1]), valid, F32) * lg)
    half = dk // 2

    def rot(x, cos, sin):
        x1, x2 = x[:, :half], x[:, half:]
        return jnp.concatenate([x1 * cos - x2 * sin, x1 * sin + x2 * cos], axis=1)

    for sub in range(n_sub):
        rows = pl.ds(sub * chunk, chunk)
        cos, sin = cos_ref[rows, :], sin_ref[rows, :]
        q = rot(q_ref[0, rows, :], cos, sin)
        k = rot(k_ref[0, rows, :], cos, sin) * (dk ** -0.5)
        v = v_ref[0, rows, :].astype(mxu_dtype)
        s = s_scr[...]
        scores = lax.dot_general(q.astype(mxu_dtype), k.astype(mxu_dtype), NT_DIMS,
                                 preferred_element_type=F32) * decay
        o = jnp.dot(scores.astype(mxu_dtype), v, preferred_element_type=F32)
        o = o + jnp.dot((q * q_scale).astype(mxu_dtype), s.astype(mxu_dtype), preferred_element_type=F32)
        kd = (k * k_scale).astype(mxu_dtype)
        s_scr[...] = s_scale * s + lax.dot_general(kd, v, (((0,), (0,)), ((), ())),
                                                   preferred_element_type=F32)
        mu = jnp.mean(o, axis=-1, keepdims=True)
        ctr = o - mu
        var = jnp.mean(ctr * ctr, axis=-1, keepdims=True)
        on = ctr * lax.rsqrt(var + GN_EPS) * gng_ref[0]
        g = g_ref[0, rows, :]
        y_ref[0, rows, :] = (g * jax.nn.sigmoid(g) * on).astype(y_ref.dtype)

    @pl.when(c == pl.num_programs(2) - 1)
    def _():
        sfin_ref[0, 0] = s_scr[...]


def _retention(proj, cos, sin, s0, gn_g3, log_g, *, batch, chunk, valid, n_sub):
    m, six_d = proj.shape
    d = six_d // 6
    t = m // batch
    dk, dv = d // RET_HEADS, 2 * d // RET_HEADS
    rows = chunk * n_sub
    proj3 = proj.reshape(batch, t, six_d)
    mxu_dtype = BF16 if chunk % 16 == 0 else F32
    kern = functools.partial(_ret_kernel, chunk=chunk, n_sub=n_sub, valid=float(valid), dk=dk,
                             mxu_dtype=mxu_dtype)
    y, s_fin = pl.pallas_call(
        kern,
        grid=(batch, RET_HEADS, t // rows),
        in_specs=[pl.BlockSpec(memory_space=pltpu.SMEM),
                  pl.BlockSpec((1, rows, dk), lambda b, h, c: (b, c, h)),
                  pl.BlockSpec((1, rows, dk), lambda b, h, c: (b, c, RET_HEADS + h)),
                  pl.BlockSpec((1, rows, dv), lambda b, h, c: (b, c, RET_HEADS + h)),
                  pl.BlockSpec((1, rows, dv), lambda b, h, c: (b, c, 2 * RET_HEADS + h)),
                  pl.BlockSpec((rows, dk // 2), lambda b, h, c: (c, 0)),
                  pl.BlockSpec((rows, dk // 2), lambda b, h, c: (c, 0)),
                  pl.BlockSpec((1, 1, dv), lambda b, h, c: (0, 0, h)),
                  pl.BlockSpec((1, 1, dk, dv), lambda b, h, c: (b, h, 0, 0))],
        out_specs=[pl.BlockSpec((1, rows, dv), lambda b, h, c: (b, c, h)),
                   pl.BlockSpec((1, 1, dk, dv), lambda b, h, c: (b, h, 0, 0))],
        out_shape=[jax.ShapeDtypeStruct((batch, t, 2 * d), mxu_dtype),
                   jax.ShapeDtypeStruct((batch, RET_HEADS, dk, dv), F32)],
        scratch_shapes=[pltpu.VMEM((dk, dv), F32)],
        compiler_params=_params("arbitrary", "arbitrary", "arbitrary"),
        name="retention",
    )(log_g, proj3, proj3, proj3, proj3, cos, sin, gn_g3, s0)
    return y.reshape(m, 2 * d).astype(BF16), s_fin


def _suffix_matrix():
    j = np.arange(2 * LANES)[:, None] % LANES
    c = np.arange(2 * LANES)[None, :]
    return jnp.asarray(np.where(c < LANES, j > c, True), dtype=BF16)


def _sb_logs(z, mask):
    t = jnp.log(1.0 + jnp.exp(-jnp.abs(z)))
    m = jnp.minimum(z, 0.0)
    log_keep = (m - z) - t
    if mask is not None:
        log_keep = jnp.where(mask, log_keep, 0.0)
    return m - t, log_keep


def _sb_suffix(log_keep, u):
    hi = log_keep.astype(BF16)
    lo = (log_keep - hi.astype(F32)).astype(BF16)
    r = jnp.dot(jnp.concatenate([hi, lo], axis=1), u, preferred_element_type=F32)
    return r[:, :LANES], r[:, LANES:]


def _sb_prompt_kernel(bias_ref, u_ref, q_ref, k_ref, v_ref, o_ref, o_scr, c_scr, *, tile):
    h = pl.program_id(1)
    qi = pl.program_id(2)
    bias = bias_ref[h]
    nb = tile // LANES
    u = u_ref[...]
    o_scr[...] = jnp.zeros_like(o_scr)
    c_scr[...] = jnp.zeros_like(c_scr)

    k0 = pl.multiple_of(qi * tile, tile)
    for kb in reversed(range(nb)):
        r0 = kb * LANES
        kk = k_ref[0, pl.ds(k0 + r0, LANES), :].astype(BF16)
        vv = v_ref[0, pl.ds(k0 + r0, LANES), :].astype(BF16)
        z = lax.dot_general(q_ref[0, r0:, :], kk, NT_DIMS, preferred_element_type=F32) + bias
        shape = (tile - r0, LANES)
        mask = lax.broadcasted_iota(jnp.int32, shape, 1) < lax.broadcasted_iota(jnp.int32, shape, 0)
        log_beta, log_keep = _sb_logs(z, mask)
        suffix, total = _sb_suffix(log_keep, u)
        carry = c_scr[r0:, :]
        w = jnp.where(mask, jnp.exp(log_beta + suffix + carry), 0.0)
        o_scr[r0:, :] += jnp.dot(w.astype(BF16), vv, preferred_element_type=F32)
        c_scr[r0:, :] = carry + total

    def left_tile(i, _):
        k0 = pl.multiple_of((qi - 1 - i) * tile, tile)
        kt = k_ref[0, pl.ds(k0, tile), :].astype(BF16)
        vt = v_ref[0, pl.ds(k0, tile), :].astype(BF16)
        z_all = lax.dot_general(q_ref[0], kt, NT_DIMS, preferred_element_type=F32) + bias
        carry = c_scr[...]
        ws = [None] * nb
        for kb in reversed(range(nb)):
            log_beta, log_keep = _sb_logs(z_all[:, kb * LANES:(kb + 1) * LANES], None)
            suffix, total = _sb_suffix(log_keep, u)
            ws[kb] = jnp.exp(log_beta + suffix + carry).astype(BF16)
            carry = carry + total
        c_scr[...] = carry
        o_scr[...] += jnp.dot(jnp.concatenate(ws, axis=1), vt, preferred_element_type=F32)
        return 0

    lax.fori_loop(0, qi, left_tile, 0)
    o_ref[0] = o_scr[...].astype(o_ref.dtype)


def _sb_prompt(q, k, v, bias, *, batch, tile=512):
    m, d = q.shape
    t = m // batch
    hd = d // SB_HEADS
    tile = min(tile, t)
    q3, k3, v3 = (a.reshape(batch, t, d) for a in (q, k, v))
    o = pl.pallas_call(
        functools.partial(_sb_prompt_kernel, tile=tile),
        grid=(batch, SB_HEADS, t // tile),
        in_specs=[pl.BlockSpec(memory_space=pltpu.SMEM),
                  pl.BlockSpec((2 * LANES, 2 * LANES), lambda b, h, i: (0, 0)),
                  pl.BlockSpec((1, tile, hd), lambda b, h, i: (b, i, h)),
                  pl.BlockSpec((1, t, hd), lambda b, h, i: (b, 0, h)),
                  pl.BlockSpec((1, t, hd), lambda b, h, i: (b, 0, h))],
        out_specs=pl.BlockSpec((1, tile, hd), lambda b, h, i: (b, i, h)),
        out_shape=jax.ShapeDtypeStruct((batch, t, d), BF16),
        scratch_shapes=[pltpu.VMEM((tile, hd), F32), pltpu.VMEM((tile, LANES), F32)],
        compiler_params=_params("arbitrary", "arbitrary", "arbitrary"),
        name="sb_prompt",
    )(bias, _suffix_matrix(), q3, k3, v3)
    return o.reshape(m, d)


def _sb_sample_kernel(pt_ref, bias_ref, u_ref, q_ref, kn_ref, vn_ref, kc_ref, vc_ref, o_ref,
                      c_scr, *, n_q):
    s = pl.program_id(1)
    rq = SAMPLE_ROWS
    hd = LANES

    @pl.when(s == 0)
    def _():
        o_ref[...] = jnp.zeros_like(o_ref)
        c_scr[...] = jnp.zeros_like(c_scr)

    def step(k_ref, v_ref, masked):
        zs = []
        for h in range(SB_HEADS):
            cols = slice(h * hd, (h + 1) * hd)
            zh = lax.dot_general(q_ref[0, :, cols].astype(BF16), k_ref[0, :, cols].astype(BF16), NT_DIMS,
                                 preferred_element_type=F32)
            zs.append(zh + bias_ref[h])
        z = jnp.concatenate(zs, axis=0)
        if masked:
            t_q = lax.broadcasted_iota(jnp.int32, z.shape, 0) % rq
            mask = lax.broadcasted_iota(jnp.int32, z.shape, 1) < jnp.minimum(t_q, n_q)
        else:
            mask = None
        log_beta, log_keep = _sb_logs(z, mask)
        suffix, total = _sb_suffix(log_keep, u_ref[...])
        carry = c_scr[...]
        w = jnp.exp(log_beta + suffix + carry)
        if masked:
            w = jnp.where(mask, w, 0.0)
        c_scr[...] = carry + total
        for h in range(SB_HEADS):
            cols = slice(h * hd, (h + 1) * hd)
            o_ref[0, :, cols] += jnp.dot(w[h * rq:(h + 1) * rq, :].astype(BF16),
                                         v_ref[0, :, cols].astype(BF16), preferred_element_type=F32)

    @pl.when(s == 0)
    def _():
        step(kn_ref, vn_ref, True)

    @pl.when(s > 0)
    def _():
        step(kc_ref, vc_ref, False)


def _sb_sample(q, k_new_page, v_new_page, cache_k, cache_v, page_table, bias, *, n_q):
    b, n_pages = page_table.shape
    d = q.shape[-1]

    def cache_map(i, s, pt):
        return (pt[i, n_pages - jnp.maximum(s, 1)], 0, 0)

    grid_spec = pltpu.PrefetchScalarGridSpec(
        num_scalar_prefetch=1,
        grid=(b, n_pages + 1),
        in_specs=[pl.BlockSpec(memory_space=pltpu.SMEM),
                  pl.BlockSpec((2 * LANES, 2 * LANES), lambda i, s, pt: (0, 0)),
                  pl.BlockSpec((1, SAMPLE_ROWS, d), lambda i, s, pt: (i, 0, 0)),
                  pl.BlockSpec((1, PAGE_SIZE, d), lambda i, s, pt: (i, 0, 0)),
                  pl.BlockSpec((1, PAGE_SIZE, d), lambda i, s, pt: (i, 0, 0)),
                  pl.BlockSpec((1, PAGE_SIZE, d), cache_map),
                  pl.BlockSpec((1, PAGE_SIZE, d), cache_map)],
        out_specs=pl.BlockSpec((1, SAMPLE_ROWS, d), lambda i, s, pt: (i, 0, 0)),
        scratch_shapes=[pltpu.VMEM((SB_HEADS * SAMPLE_ROWS, LANES), F32)],
    )
    return pl.pallas_call(
        functools.partial(_sb_sample_kernel, n_q=n_q),
        grid_spec=grid_spec,
        out_shape=jax.ShapeDtypeStruct((b, SAMPLE_ROWS, d), F32),
        compiler_params=_params("arbitrary", "arbitrary"),
        name="sb_sample",
    )(page_table, bias, _suffix_matrix(), q, k_new_page, v_new_page, cache_k, cache_v)


def _rope_tables(pos, half):
    freqs = ROPE_BASE ** (-jnp.arange(half, dtype=F32) / half)
    ang = pos.astype(F32)[:, None] * freqs[None, :]
    return jnp.cos(ang), jnp.sin(ang)


def _trunk(x, mods, groups, weights, retention_fn, attention_fn, tm):
    (g_pre_mix, g_post_mix, g_pre_ffn, g_post_ffn, w_ff1, w_ff2, a_w_in, a_w_out, kv_norm_g, w_kv,
     b_w_q, b_w_out) = weights
    d = x.shape[1]
    n_a = a_w_in.shape[0]
    depth = w_ff1.shape[0]
    hd = d // SB_HEADS
    states = []
    k_new = v_new = None
    h = _norm(x, g_pre_mix, 0, mods[0][1], mods[0][0], groups)
    hn = None
    for layer in range(depth):
        sh1, sc1, gt1, sh2, sc2, gt2 = mods[layer]
        if layer < n_a:
            proj = _mm(h, a_w_in, layer, out_dtype=F32, tm=tm)
            y, s_fin = retention_fn(layer, proj)
            states.append(s_fin)
            w_out, j = a_w_out, layer
        else:
            if k_new is None:
                k_new = _mm(hn, w_kv, 0, out_dtype=F32, n_out=d, col0=0, tm=tm)
                v_new = _mm(hn, w_kv, 0, out_dtype=F32, n_out=d, col0=d, tm=tm)
            j = layer - n_a
            q = _mm(h, b_w_q, j, out_dtype=BF16, epilogue="scale", scale=hd ** -0.5, tm=tm)
            y = attention_fn(j, q, k_new, v_new)
            w_out = b_w_out
        x, h = _mm(y, w_out, j, out_dtype=F32, epilogue="resid", resid=(x, gt1, g_post_mix, layer),
                   nexts=((g_pre_ffn, layer, sc2, sh2),), groups=groups, tm=tm)
        f = _mm(h, w_ff1, layer, out_dtype=BF16, epilogue="relu2", tm=tm)
        nexts = []
        if layer + 1 < depth:
            nexts.append((g_pre_mix, layer + 1, mods[layer + 1][1], mods[layer + 1][0]))
        if layer + 1 == n_a:
            nexts.append((kv_norm_g, 0, None, None))
        outs = _mm(f, w_ff2, layer, out_dtype=F32, epilogue="resid", resid=(x, gt2, g_post_ffn, layer),
                   nexts=tuple(nexts), groups=groups, tm=tm)
        x = outs[0]
        if layer + 1 < depth:
            h = outs[1]
        if layer + 1 == n_a:
            hn = outs[-1]
    return x, jnp.stack(states, axis=0), k_new, v_new


def kernel(x_prompt, x_sample, c_prompt, c_sample, state_ret, cache_k, cache_v, page_table, w_ada, b_ada,
           g_pre_mix, g_post_mix, g_pre_ffn, g_post_ffn, w_ff1, w_ff2, a_w_in, a_w_out, a_gn_g, kv_norm_g,
           w_kv, b_w_q, b_w_out, b_sb):
    bp, seq, d = x_prompt.shape
    bs, dec_seq, _ = x_sample.shape
    depth = w_ada.shape[0]
    n_pool, page, _, _ = cache_k.shape
    n_pages = page_table.shape[1]
    past_len = n_pages * page
    hd = d // SB_HEADS
    dk, dv = d // RET_HEADS, 2 * d // RET_HEADS
    rs = SAMPLE_ROWS

    weights = (g_pre_mix.reshape(depth, 1, d), g_post_mix.reshape(depth, 1, d),
               g_pre_ffn.reshape(depth, 1, d), g_post_ffn.reshape(depth, 1, d),
               w_ff1.astype(BF16), w_ff2.astype(BF16), a_w_in.astype(BF16), a_w_out.astype(BF16),
               kv_norm_g.reshape(1, 1, d), w_kv.astype(BF16)[None], b_w_q.astype(BF16), b_w_out.astype(BF16))
    gn_g3 = a_gn_g.reshape(a_gn_g.shape[0], 1, 2 * d)
    log_g = jnp.log1p(-jnp.exp2(-5.0 - jnp.arange(RET_HEADS, dtype=F32)))

    n_c = bp + bs
    c_rows = -(-n_c // SUBLANES) * SUBLANES
    c_all = jnp.concatenate([c_prompt, c_sample, jnp.zeros((c_rows - n_c, d), F32)], axis=0)
    mods_all = _ada(c_all, w_ada, b_ada)
    mods_p = [[m[:, None, :] for m in jnp.split(mods_all[l, :bp], 6, axis=-1)] for l in range(depth)]
    mods_s = [[jnp.repeat(m, rs, axis=0)[None] for m in jnp.split(mods_all[l, bp:n_c], 6, axis=-1)]
              for l in range(depth)]

    cos_p, sin_p = _rope_tables(jnp.arange(seq, dtype=jnp.int32), dk // 2)
    s0_p = jnp.zeros((bp, RET_HEADS, dk, dv), F32)

    def ret_prompt(layer, proj):
        return _retention(proj, cos_p, sin_p, s0_p, gn_g3[layer:layer + 1], log_g, batch=bp,
                          chunk=RET_CHUNK, valid=RET_CHUNK, n_sub=4)

    def attn_prompt(j, q, k_new, v_new):
        return _sb_prompt(q, k_new, v_new, b_sb[j], batch=bp)

    y_p, st_p, k_p, v_p = _trunk(x_prompt.reshape(bp * seq, d), mods_p, bp, weights, ret_prompt,
                                 attn_prompt, 1024)

    x_s = jnp.pad(x_sample, ((0, 0), (0, rs - dec_seq), (0, 0))).reshape(bs * rs, d)
    cos_s, sin_s = _rope_tables(past_len + jnp.arange(rs, dtype=jnp.int32), dk // 2)
    cache_k3 = cache_k.reshape(n_pool, page, d)
    cache_v3 = cache_v.reshape(n_pool, page, d)

    def ret_sample(layer, proj):
        return _retention(proj, cos_s, sin_s, state_ret[layer], gn_g3[layer:layer + 1], log_g, batch=bs,
                          chunk=rs, valid=dec_seq, n_sub=1)

    def attn_sample(j, q, k_new, v_new):
        pad = ((0, 0), (0, page - rs), (0, 0))
        kn = jnp.pad(k_new.reshape(bs, rs, d), pad)
        vn = jnp.pad(v_new.reshape(bs, rs, d), pad)
        o = _sb_sample(q.astype(F32).reshape(bs, rs, d), kn, vn, cache_k3, cache_v3, page_table, b_sb[j],
                       n_q=dec_seq)
        return o.reshape(bs * rs, d).astype(BF16)

    y_s, st_s, k_s, v_s = _trunk(x_s, mods_s, 1, weights, ret_sample, attn_sample, bs * rs)

    def unpad(a, *tail):
        return a.reshape(bs, rs, *tail)[:, :dec_seq]

    return (y_p.reshape(bp, seq, d), unpad(y_s, d), st_p, st_s,
            k_p.reshape(bp, seq, SB_HEADS, hd), v_p.reshape(bp, seq, SB_HEADS, hd),
            unpad(k_s, SB_HEADS, hd), unpad(v_s, SB_HEADS, hd))
```

```python
import functools

import jax
import jax.numpy as jnp
import numpy as np
from jax import lax
from jax.experimental import pallas as pl
from jax.experimental.pallas import tpu as pltpu

F32 = jnp.float32
BF16 = jnp.bfloat16

LANES = 128
SUBLANES = 8
VMEM_LIMIT_BYTES = 56 * 1024 * 1024

RET_HEADS = 8
RET_CHUNK = 128
ROPE_BASE = 10000.0
SB_HEADS = 16
PAGE_SIZE = 128
NORM_EPS = 1e-6
GN_EPS = 1e-6
SAMPLE_ROWS = SUBLANES
NT_DIMS = (((1,), (1,)), ((), ()))


def _params(*sem):
    return pltpu.CompilerParams(dimension_semantics=sem, vmem_limit_bytes=VMEM_LIMIT_BYTES)


def _mod_spec(mod, tile_rows, rows_per_group):
    d = mod.shape[-1]
    if mod.shape[1] == 1:
        return pl.BlockSpec((1, 1, d), lambda *g: ((g[0] * tile_rows) // rows_per_group, 0, 0))
    return pl.BlockSpec((1, tile_rows, d), lambda *g: (0, g[0], 0))


def _rms(v, g):
    return v * lax.rsqrt(jnp.mean(v * v, axis=-1, keepdims=True) + NORM_EPS) * g


def _ada_kernel(c_ref, w_ref, b_ref, o_ref):
    c = c_ref[...]
    a = (c * jax.nn.sigmoid(c)).astype(BF16)
    o_ref[0] = jnp.dot(a, w_ref[0].astype(BF16), preferred_element_type=F32) + b_ref[0]


def _ada(c, w_ada, b_ada, tn=1024):
    depth, d, n = w_ada.shape
    m = c.shape[0]
    return pl.pallas_call(
        _ada_kernel,
        grid=(depth, n // tn),
        in_specs=[pl.BlockSpec((m, d), lambda l, j: (0, 0)),
                  pl.BlockSpec((1, d, tn), lambda l, j: (l, 0, j)),
                  pl.BlockSpec((1, 1, tn), lambda l, j: (l, 0, j))],
        out_specs=pl.BlockSpec((1, m, tn), lambda l, j: (l, 0, j)),
        out_shape=jax.ShapeDtypeStruct((depth, m, n), F32),
        compiler_params=_params("arbitrary", "arbitrary"),
        name="ada_mods",
    )(c, w_ada, b_ada.reshape(depth, 1, n))


def _norm_kernel(x_ref, g_ref, sc_ref, sh_ref, o_ref):
    y = _rms(x_ref[...], g_ref[0]) * (1.0 + sc_ref[0]) + sh_ref[0]
    o_ref[...] = y.astype(o_ref.dtype)


def _norm(x, g3, layer, sc, sh, groups, tr=512):
    m, d = x.shape
    tr = min(tr, m)
    return pl.pallas_call(
        _norm_kernel,
        grid=(m // tr,),
        in_specs=[pl.BlockSpec((tr, d), lambda i: (i, 0)),
                  pl.BlockSpec((1, 1, d), lambda i: (layer, 0, 0)),
                  _mod_spec(sc, tr, m // groups), _mod_spec(sh, tr, m // groups)],
        out_specs=pl.BlockSpec((tr, d), lambda i: (i, 0)),
        out_shape=jax.ShapeDtypeStruct((m, d), BF16),
        compiler_params=_params("arbitrary"),
        name="norm_mod",
    )(x, g3, sc, sh)


def _mm_kernel(*refs, epilogue, nk, nexts, scale):
    a_ref, w_ref = refs[:2]
    pos = 2
    next_in = []
    if epilogue == "resid":
        x_ref, gt_ref, g_ref = refs[2:5]
        pos = 5
        for modulated in nexts:
            n_in = 3 if modulated else 1
            next_in.append(refs[pos:pos + n_in])
            pos += n_in
    o_ref = refs[pos]
    next_out = refs[pos + 1:pos + 1 + len(nexts)]
    acc_ref = refs[pos + 1 + len(nexts)] if nk > 1 else None

    def prod():
        return jnp.dot(a_ref[...], w_ref[0], preferred_element_type=F32)

    def finish(acc):
        if epilogue == "relu2":
            r = jnp.maximum(acc, 0.0)
            acc = r * r
        elif epilogue == "scale":
            acc = acc * scale
        elif epilogue == "resid":
            acc = x_ref[...] + gt_ref[0] * _rms(acc, g_ref[0])
            for ins, out in zip(next_in, next_out):
                y = _rms(acc, ins[0][0])
                if len(ins) == 3:
                    y = y * (1.0 + ins[1][0]) + ins[2][0]
                out[...] = y.astype(out.dtype)
        o_ref[...] = acc.astype(o_ref.dtype)

    if nk == 1:
        finish(prod())
        return
    k = pl.program_id(2)

    @pl.when(k == 0)
    def _():
        acc_ref[...] = prod()

    @pl.when(jnp.logical_and(k > 0, k < nk - 1))
    def _():
        acc_ref[...] += prod()

    @pl.when(k == nk - 1)
    def _():
        finish(acc_ref[...] + prod())


def _mm(a, w3, layer, *, out_dtype, n_out=None, col0=0, epilogue="none", scale=1.0, resid=None, nexts=(),
        groups=1, tm=1024, tn=1024, tk=2048):
    m, kdim = a.shape
    n = w3.shape[2] if n_out is None else n_out
    tm, tn, tk = min(tm, m), min(tn, n), min(tk, kdim)
    if epilogue == "resid":
        tm, tn = min(tm, 512), n
    nk = kdim // tk
    cb = col0 // tn
    rpg = m // groups
    in_specs = [pl.BlockSpec((tm, tk), lambda i, j, k: (i, k)),
                pl.BlockSpec((1, tk, tn), lambda i, j, k: (layer, k, j + cb))]
    args = [a, w3]
    out_specs = [pl.BlockSpec((tm, tn), lambda i, j, k: (i, j))]
    out_shape = [jax.ShapeDtypeStruct((m, n), out_dtype)]
    if epilogue == "resid":
        x, gt, g3, glayer = resid
        in_specs += [pl.BlockSpec((tm, n), lambda i, j, k: (i, 0)), _mod_spec(gt, tm, rpg),
                     pl.BlockSpec((1, 1, n), lambda i, j, k: (glayer, 0, 0))]
        args += [x, gt, g3]
        for ng3, nlayer, sc, sh in nexts:
            in_specs.append(pl.BlockSpec((1, 1, n), lambda i, j, k, nlayer=nlayer: (nlayer, 0, 0)))
            args.append(ng3)
            if sc is not None:
                in_specs += [_mod_spec(sc, tm, rpg), _mod_spec(sh, tm, rpg)]
                args += [sc, sh]
            out_specs.append(pl.BlockSpec((tm, n), lambda i, j, k: (i, 0)))
            out_shape.append(jax.ShapeDtypeStruct((m, n), BF16))
    outs = pl.pallas_call(
        functools.partial(_mm_kernel, epilogue=epilogue, nk=nk, scale=scale,
                          nexts=tuple(sc is not None for _, _, sc, _ in nexts)),
        grid=(m // tm, n // tn, nk),
        in_specs=in_specs,
        out_specs=out_specs,
        out_shape=out_shape,
        scratch_shapes=[pltpu.VMEM((tm, tn), F32)] if nk > 1 else [],
        compiler_params=_params("arbitrary", "arbitrary", "arbitrary"),
        name="mm_" + epilogue,
    )(*args)
    return outs if epilogue == "resid" else outs[0]


def _ret_kernel(lg_ref, q_ref, k_ref, v_ref, g_ref, cos_ref, sin_ref, gng_ref, s0_ref,
                y_ref, sfin_ref, s_scr, *, chunk, n_sub, valid, dk, mxu_dtype):
    h = pl.program_id(1)
    c = pl.program_id(2)

    @pl.when(c == 0)
    def _():
        s_scr[...] = s0_ref[0, 0]

    lg = lg_ref[h]
    row = lax.broadcasted_iota(jnp.int32, (chunk, 1), 0).astype(F32)
    col = lax.broadcasted_iota(jnp.int32, (1, chunk), 1).astype(F32)
    diff = row - col
    causal = diff >= 0.0
    decay = jnp.where(causal, jnp.exp(jnp.where(causal, diff, 0.0) * lg), 0.0)
    q_scale = jnp.exp((row + 1.0) * lg)
    k_scale = jnp.where(row < valid, jnp.exp((valid - 1.0 - row) * lg), 0.0)
    s_scale = jnp.exp(jnp.full((1, s_scr.shape[1]), valid, F32) * lg)
    half = dk // 2

    def rot(x, cos, sin):
        x1, x2 = x[:, :half], x[:, half:]
        return jnp.concatenate([x1 * cos - x2 * sin, x1 * sin + x2 * cos], axis=1)

    for sub in range(n_sub):
        rows = pl.ds(sub * chunk, chunk)
        cos, sin = cos_ref[rows, :], sin_ref[rows, :]
        q = rot(q_ref[0, rows, :].astype(F32), cos, sin)
        k = rot(k_ref[0, rows, :].astype(F32), cos, sin) * (dk ** -0.5)
        v = v_ref[0, rows, :].astype(mxu_dtype)
        s = s_scr[...]
        scores = lax.dot_general(q.astype(mxu_dtype), k.astype(mxu_dtype), NT_DIMS,
                                 preferred_element_type=F32) * decay
        o = jnp.dot(scores.astype(mxu_dtype), v, preferred_element_type=F32)
        o = o + jnp.dot((q * q_scale).astype(mxu_dtype), s.astype(mxu_dtype), preferred_element_type=F32)
        kd = (k * k_scale).astype(mxu_dtype)
        s_scr[...] = s_scale * s + lax.dot_general(kd, v, (((0,), (0,)), ((), ())),
                                                   preferred_element_type=F32)
        mu = jnp.mean(o, axis=-1, keepdims=True)
        ctr = o - mu
        var = jnp.mean(ctr * ctr, axis=-1, keepdims=True)
        on = ctr * lax.rsqrt(var + GN_EPS) * gng_ref[0]
        g = g_ref[0, rows, :].astype(F32)
        y_ref[0, rows, :] = (g * jax.nn.sigmoid(g) * on).astype(y_ref.dtype)

    @pl.when(c == pl.num_programs(2) - 1)
    def _():
        sfin_ref[0, 0] = s_scr[...]


def _retention(proj, cos, sin, s0, gn_g3, log_g, *, batch, chunk, valid, n_sub):
    m, six_d = proj.shape
    d = six_d // 6
    t = m // batch
    dk, dv = d // RET_HEADS, 2 * d // RET_HEADS
    rows = chunk * n_sub
    proj3 = proj.reshape(batch, t, six_d)
    mxu_dtype = BF16 if chunk % 16 == 0 else F32
    kern = functools.partial(_ret_kernel, chunk=chunk, n_sub=n_sub, valid=float(valid), dk=dk,
                             mxu_dtype=mxu_dtype)
    y, s_fin = pl.pallas_call(
        kern,
        grid=(batch, RET_HEADS, t // rows),
        in_specs=[pl.BlockSpec(memory_space=pltpu.SMEM),
                  pl.BlockSpec((1, rows, dk), lambda b, h, c: (b, c, h)),
                  pl.BlockSpec((1, rows, dk), lambda b, h, c: (b, c, RET_HEADS + h)),
                  pl.BlockSpec((1, rows, dv), lambda b, h, c: (b, c, RET_HEADS + h)),
                  pl.BlockSpec((1, rows, dv), lambda b, h, c: (b, c, 2 * RET_HEADS + h)),
                  pl.BlockSpec((rows, dk // 2), lambda b, h, c: (c, 0)),
                  pl.BlockSpec((rows, dk // 2), lambda b, h, c: (c, 0)),
                  pl.BlockSpec((1, 1, dv), lambda b, h, c: (0, 0, h)),
                  pl.BlockSpec((1, 1, dk, dv), lambda b, h, c: (b, h, 0, 0))],
        out_specs=[pl.BlockSpec((1, rows, dv), lambda b, h, c: (b, c, h)),
                   pl.BlockSpec((1, 1, dk, dv), lambda b, h, c: (b, h, 0, 0))],
        out_shape=[jax.ShapeDtypeStruct((batch, t, 2 * d), mxu_dtype),
                   jax.ShapeDtypeStruct((batch, RET_HEADS, dk, dv), F32)],
        scratch_shapes=[pltpu.VMEM((dk, dv), F32)],
        compiler_params=_params("arbitrary", "arbitrary", "arbitrary"),
        name="retention",
    )(log_g, proj3, proj3, proj3, proj3, cos, sin, gn_g3, s0)
    return y.reshape(m, 2 * d).astype(BF16), s_fin


def _suffix_matrix():
    j = np.arange(LANES)[:, None]
    c = np.arange(2 * LANES)[None, :]
    return jnp.asarray(np.where(c < LANES, j > c, True), dtype=BF16)


def _sb_logs(z):
    m = jnp.minimum(z, 0.0)
    mz = m - z
    t = jnp.log(1.0 + jnp.exp(m + mz))
    return m - t, mz - t


def _mask_top(x, mask):
    top = jnp.where(mask, x[:LANES], 0.0)
    return top if x.shape[0] == LANES else jnp.concatenate([top, x[LANES:]], axis=0)


def _sb_suffix(log_keep, u):
    r = jnp.dot(log_keep.astype(BF16), u, preferred_element_type=F32)
    return r[:, :LANES], r[:, LANES:]


def _sb_prompt_kernel(bias_ref, u_ref, q_ref, k_ref, v_ref, o_ref, o_scr, c_scr, *, tq, tk):
    h = pl.program_id(1)
    qi = pl.program_id(2)
    u = u_ref[...]
    o_scr[...] = jnp.zeros_like(o_scr)
    c_scr[...] = jnp.zeros_like(c_scr)

    lane = lax.broadcasted_iota(jnp.int32, (1, LANES), 1)
    b0 = jnp.full((1, LANES), bias_ref[h], F32)
    b1 = b0.astype(BF16).astype(F32)
    b2 = (b0 - b1).astype(BF16).astype(F32)
    b3 = (b0 - b1) - b2
    k_ext = jnp.where(lane == 0, b1, jnp.where(lane == 1, b2, jnp.where(lane == 2, b3, 0.0)))
    q_ext = jnp.where(lane < 3, 1.0, 0.0)
    q_aug = jnp.concatenate([q_ref[0], jnp.broadcast_to(q_ext, (tq, LANES)).astype(BF16)], axis=1)

    def keys(k0, n):
        kk = k_ref[0, pl.ds(k0, n), :].astype(BF16)
        return jnp.concatenate([kk, jnp.broadcast_to(k_ext, (n, LANES)).astype(BF16)], axis=1)

    tri = (lax.broadcasted_iota(jnp.int32, (LANES, LANES), 1)
           < lax.broadcasted_iota(jnp.int32, (LANES, LANES), 0))
    k0 = pl.multiple_of(qi * tq, tq)
    for kb in reversed(range(tq // LANES)):
        r0 = kb * LANES
        vv = v_ref[0, pl.ds(k0 + r0, LANES), :].astype(BF16)
        z = lax.dot_general(q_aug[r0:], keys(k0 + r0, LANES), NT_DIMS, preferred_element_type=F32)
        log_beta, log_keep = _sb_logs(z)
        suffix, total = _sb_suffix(_mask_top(log_keep, tri), u)
        carry = c_scr[r0:, :]
        w = _mask_top(jnp.exp(log_beta + suffix + carry), tri)
        o_scr[r0:, :] += jnp.dot(w.astype(BF16), vv, preferred_element_type=F32)
        c_scr[r0:, :] = carry + total

    nb = tk // LANES

    def left_tile(i, _):
        k0 = pl.multiple_of(qi * tq - (i + 1) * tk, tk)
        vt = v_ref[0, pl.ds(k0, tk), :].astype(BF16)
        z_all = lax.dot_general(q_aug, keys(k0, tk), NT_DIMS, preferred_element_type=F32)
        carry = c_scr[...]
        ws = [None] * nb
        for kb in reversed(range(nb)):
            log_beta, log_keep = _sb_logs(z_all[:, kb * LANES:(kb + 1) * LANES])
            suffix, total = _sb_suffix(log_keep, u)
            ws[kb] = jnp.exp(log_beta + suffix + carry).astype(BF16)
            carry = carry + total
        c_scr[...] = carry
        o_scr[...] += jnp.dot(jnp.concatenate(ws, axis=1), vt, preferred_element_type=F32)
        return 0

    lax.fori_loop(0, qi * (tq // tk), left_tile, 0)
    o_ref[0] = o_scr[...].astype(o_ref.dtype)


def _sb_prompt(q, k, v, bias, *, batch, tq=512, tk=512):
    m, d = q.shape
    t = m // batch
    hd = d // SB_HEADS
    tq = min(tq, t)
    tk = min(tk, tq)
    q3, k3, v3 = (a.reshape(batch, t, d) for a in (q, k, v))
    o = pl.pallas_call(
        functools.partial(_sb_prompt_kernel, tq=tq, tk=tk),
        grid=(batch, SB_HEADS, t // tq),
        in_specs=[pl.BlockSpec(memory_space=pltpu.SMEM),
                  pl.BlockSpec((LANES, 2 * LANES), lambda b, h, i: (0, 0)),
                  pl.BlockSpec((1, tq, hd), lambda b, h, i: (b, i, h)),
                  pl.BlockSpec((1, t, hd), lambda b, h, i: (b, 0, h)),
                  pl.BlockSpec((1, t, hd), lambda b, h, i: (b, 0, h))],
        out_specs=pl.BlockSpec((1, tq, hd), lambda b, h, i: (b, i, h)),
        out_shape=jax.ShapeDtypeStruct((batch, t, d), BF16),
        scratch_shapes=[pltpu.VMEM((tq, hd), F32), pltpu.VMEM((tq, LANES), F32)],
        compiler_params=_params("arbitrary", "arbitrary", "arbitrary"),
        name="sb_prompt",
    )(bias, _suffix_matrix(), q3, k3, v3)
    return o.reshape(m, d)


def _sb_sample_kernel(pt_ref, bias_ref, u_ref, q_ref, kn_ref, vn_ref, *refs, n_q, pps):
    kc_refs, vc_refs = refs[:pps], refs[pps:2 * pps]
    o_ref, c_scr = refs[2 * pps:]
    s = pl.program_id(1)
    rq = SAMPLE_ROWS
    hd = LANES

    @pl.when(s == 0)
    def _():
        o_ref[...] = jnp.zeros_like(o_ref)
        c_scr[...] = jnp.zeros_like(c_scr)

    def head_rows(ref, h):
        return ref[pl.ds(h, PAGE_SIZE, stride=SB_HEADS), :].astype(BF16)

    def page(k_ref, v_ref, masked):
        zs = []
        for h in range(SB_HEADS):
            qh = q_ref[0, :, h * hd:(h + 1) * hd].astype(BF16)
            zh = lax.dot_general(qh, head_rows(k_ref, h), NT_DIMS, preferred_element_type=F32)
            zs.append(zh + bias_ref[h])
        z = jnp.concatenate(zs, axis=0)
        log_beta, log_keep = _sb_logs(z)
        if masked:
            t_q = lax.broadcasted_iota(jnp.int32, z.shape, 0) % rq
            mask = lax.broadcasted_iota(jnp.int32, z.shape, 1) < jnp.minimum(t_q, n_q)
            log_keep = jnp.where(mask, log_keep, 0.0)
        suffix, total = _sb_suffix(log_keep, u_ref[...])
        carry = c_scr[...]
        w = jnp.exp(log_beta + suffix + carry)
        if masked:
            w = jnp.where(mask, w, 0.0)
        c_scr[...] = carry + total
        for h in range(SB_HEADS):
            o_ref[0, :, h * hd:(h + 1) * hd] += jnp.dot(w[h * rq:(h + 1) * rq, :].astype(BF16),
                                                       head_rows(v_ref, h), preferred_element_type=F32)

    @pl.when(s == 0)
    def _():
        page(kn_ref, vn_ref, True)

    @pl.when(s > 0)
    def _():
        for p in range(pps):
            page(kc_refs[p], vc_refs[p], False)


def _sb_sample(q, k_new_page, v_new_page, cache_k, cache_v, page_table, bias, *, n_q):
    b, n_pages = page_table.shape
    d = q.shape[-1]
    rows, hd = cache_k.shape[1:]
    pps = next(p for p in (4, 2, 1) if n_pages % p == 0)

    def cache_spec(p):
        def index_map(i, s, pt):
            return (pt[i, n_pages - 1 - (jnp.maximum(s, 1) - 1) * pps - p], 0, 0)
        return pl.BlockSpec((None, rows, hd), index_map)

    new_spec = pl.BlockSpec((None, rows, hd), lambda i, s, pt: (i, 0, 0))
    grid_spec = pltpu.PrefetchScalarGridSpec(
        num_scalar_prefetch=1,
        grid=(b, n_pages // pps + 1),
        in_specs=[pl.BlockSpec(memory_space=pltpu.SMEM),
                  pl.BlockSpec((LANES, 2 * LANES), lambda i, s, pt: (0, 0)),
                  pl.BlockSpec((1, SAMPLE_ROWS, d), lambda i, s, pt: (i, 0, 0)),
                  new_spec, new_spec] + [cache_spec(p) for p in range(pps)] * 2,
        out_specs=pl.BlockSpec((1, SAMPLE_ROWS, d), lambda i, s, pt: (i, 0, 0)),
        scratch_shapes=[pltpu.VMEM((SB_HEADS * SAMPLE_ROWS, LANES), F32)],
    )
    return pl.pallas_call(
        functools.partial(_sb_sample_kernel, n_q=n_q, pps=pps),
        grid_spec=grid_spec,
        out_shape=jax.ShapeDtypeStruct((b, SAMPLE_ROWS, d), F32),
        compiler_params=_params("arbitrary", "arbitrary"),
        name="sb_sample",
    )(page_table, bias, _suffix_matrix(), q, k_new_page, v_new_page,
      *([cache_k] * pps), *([cache_v] * pps))


def _rope_tables(pos, half):
    freqs = ROPE_BASE ** (-jnp.arange(half, dtype=F32) / half)
    ang = pos.astype(F32)[:, None] * freqs[None, :]
    return jnp.cos(ang), jnp.sin(ang)


def _trunk(x, mods, groups, weights, retention_fn, attention_fn, tm, proj_dtype):
    (g_pre_mix, g_post_mix, g_pre_ffn, g_post_ffn, w_ff1, w_ff2, a_w_in, a_w_out, kv_norm_g, w_kv,
     b_w_q, b_w_out) = weights
    d = x.shape[1]
    n_a = a_w_in.shape[0]
    depth = w_ff1.shape[0]
    hd = d // SB_HEADS
    states = []
    k_new = v_new = None
    h = _norm(x, g_pre_mix, 0, mods[0][1], mods[0][0], groups)
    hn = None
    for layer in range(depth):
        sh1, sc1, gt1, sh2, sc2, gt2 = mods[layer]
        if layer < n_a:
            proj = _mm(h, a_w_in, layer, out_dtype=proj_dtype, tm=tm)
            y, s_fin = retention_fn(layer, proj)
            states.append(s_fin)
            w_out, j = a_w_out, layer
        else:
            if k_new is None:
                k_new = _mm(hn, w_kv, 0, out_dtype=F32, n_out=d, col0=0, tm=tm)
                v_new = _mm(hn, w_kv, 0, out_dtype=F32, n_out=d, col0=d, tm=tm)
            j = layer - n_a
            q = _mm(h, b_w_q, j, out_dtype=BF16, epilogue="scale", scale=hd ** -0.5, tm=tm)
            y = attention_fn(j, q, k_new, v_new)
            w_out = b_w_out
        x, h = _mm(y, w_out, j, out_dtype=F32, epilogue="resid", resid=(x, gt1, g_post_mix, layer),
                   nexts=((g_pre_ffn, layer, sc2, sh2),), groups=groups, tm=tm)
        f = _mm(h, w_ff1, layer, out_dtype=BF16, epilogue="relu2", tm=tm)
        nexts = []
        if layer + 1 < depth:
            nexts.append((g_pre_mix, layer + 1, mods[layer + 1][1], mods[layer + 1][0]))
        if layer + 1 == n_a:
            nexts.append((kv_norm_g, 0, None, None))
        outs = _mm(f, w_ff2, layer, out_dtype=F32, epilogue="resid", resid=(x, gt2, g_post_ffn, layer),
                   nexts=tuple(nexts), groups=groups, tm=tm)
        x = outs[0]
        if layer + 1 < depth:
            h = outs[1]
        if layer + 1 == n_a:
            hn = outs[-1]
    return x, jnp.stack(states, axis=0), k_new, v_new


def kernel(x_prompt, x_sample, c_prompt, c_sample, state_ret, cache_k, cache_v, page_table, w_ada, b_ada,
           g_pre_mix, g_post_mix, g_pre_ffn, g_post_ffn, w_ff1, w_ff2, a_w_in, a_w_out, a_gn_g, kv_norm_g,
           w_kv, b_w_q, b_w_out, b_sb):
    bp, seq, d = x_prompt.shape
    bs, dec_seq, _ = x_sample.shape
    depth = w_ada.shape[0]
    n_pool, page, _, _ = cache_k.shape
    n_pages = page_table.shape[1]
    past_len = n_pages * page
    hd = d // SB_HEADS
    dk, dv = d // RET_HEADS, 2 * d // RET_HEADS
    rs = SAMPLE_ROWS

    weights = (g_pre_mix.reshape(depth, 1, d), g_post_mix.reshape(depth, 1, d),
               g_pre_ffn.reshape(depth, 1, d), g_post_ffn.reshape(depth, 1, d),
               w_ff1.astype(BF16), w_ff2.astype(BF16), a_w_in.astype(BF16), a_w_out.astype(BF16),
               kv_norm_g.reshape(1, 1, d), w_kv.astype(BF16)[None], b_w_q.astype(BF16), b_w_out.astype(BF16))
    gn_g3 = a_gn_g.reshape(a_gn_g.shape[0], 1, 2 * d)
    log_g = jnp.log1p(-jnp.exp2(-5.0 - jnp.arange(RET_HEADS, dtype=F32)))

    n_c = bp + bs
    c_rows = -(-n_c // SUBLANES) * SUBLANES
    c_all = jnp.concatenate([c_prompt, c_sample, jnp.zeros((c_rows - n_c, d), F32)], axis=0)
    mods_all = _ada(c_all, w_ada, b_ada)
    mods_p = [[m[:, None, :] for m in jnp.split(mods_all[l, :bp], 6, axis=-1)] for l in range(depth)]
    mods_s = [[jnp.repeat(m, rs, axis=0)[None] for m in jnp.split(mods_all[l, bp:n_c], 6, axis=-1)]
              for l in range(depth)]

    cos_p, sin_p = _rope_tables(jnp.arange(seq, dtype=jnp.int32), dk // 2)
    s0_p = jnp.zeros((bp, RET_HEADS, dk, dv), F32)

    def ret_prompt(layer, proj):
        return _retention(proj, cos_p, sin_p, s0_p, gn_g3[layer:layer + 1], log_g, batch=bp,
                          chunk=RET_CHUNK, valid=RET_CHUNK, n_sub=4)

    def attn_prompt(j, q, k_new, v_new):
        return _sb_prompt(q, k_new, v_new, b_sb[j], batch=bp)

    y_p, st_p, k_p, v_p = _trunk(x_prompt.reshape(bp * seq, d), mods_p, bp, weights, ret_prompt,
                                 attn_prompt, 1024, BF16)

    x_s = jnp.pad(x_sample, ((0, 0), (0, rs - dec_seq), (0, 0))).reshape(bs * rs, d)
    cos_s, sin_s = _rope_tables(past_len + jnp.arange(rs, dtype=jnp.int32), dk // 2)
    cache_k3 = cache_k.reshape(n_pool, page * SB_HEADS, hd)
    cache_v3 = cache_v.reshape(n_pool, page * SB_HEADS, hd)

    def ret_sample(layer, proj):
        return _retention(proj, cos_s, sin_s, state_ret[layer], gn_g3[layer:layer + 1], log_g, batch=bs,
                          chunk=rs, valid=dec_seq, n_sub=1)

    def attn_sample(j, q, k_new, v_new):
        pad = ((0, 0), (0, (page - rs) * SB_HEADS), (0, 0))
        kn = jnp.pad(k_new.reshape(bs, rs * SB_HEADS, hd), pad)
        vn = jnp.pad(v_new.reshape(bs, rs * SB_HEADS, hd), pad)
        o = _sb_sample(q.astype(F32).reshape(bs, rs, d), kn, vn, cache_k3, cache_v3, page_table, b_sb[j],
                       n_q=dec_seq)
        return o.reshape(bs * rs, d).astype(BF16)

    y_s, st_s, k_s, v_s = _trunk(x_s, mods_s, 1, weights, ret_sample, attn_sample, bs * rs, F32)

    def unpad(a, *tail):
        return a.reshape(bs, rs, *tail)[:, :dec_seq]

    return (y_p.reshape(bp, seq, d), unpad(y_s, d), st_p, st_s,
            k_p.reshape(bp, seq, SB_HEADS, hd), v_p.reshape(bp, seq, SB_HEADS, hd),
            unpad(k_s, SB_HEADS, hd), unpad(v_s, SB_HEADS, hd))
```

```python
import functools

import jax
import jax.numpy as jnp
import numpy as np
from jax import lax
from jax.experimental import pallas as pl
from jax.experimental.pallas import tpu as pltpu

F32 = jnp.float32
BF16 = jnp.bfloat16

LANES = 128
SUBLANES = 8
VMEM_LIMIT_BYTES = 56 * 1024 * 1024

RET_HEADS = 8
RET_CHUNK = 128
ROPE_BASE = 10000.0
SB_HEADS = 16
PAGE_SIZE = 128
NORM_EPS = 1e-6
GN_EPS = 1e-6
SAMPLE_ROWS = SUBLANES
NT_DIMS = (((1,), (1,)), ((), ()))


def _params(*sem):
    return pltpu.CompilerParams(dimension_semantics=sem, vmem_limit_bytes=VMEM_LIMIT_BYTES)


def _mod_spec(mod, tile_rows, rows_per_group):
    d = mod.shape[-1]
    if mod.shape[1] == 1:
        return pl.BlockSpec((1, 1, d), lambda *g: ((g[0] * tile_rows) // rows_per_group, 0, 0))
    return pl.BlockSpec((1, tile_rows, d), lambda *g: (0, g[0], 0))


def _rms(v, g):
    return v * lax.rsqrt(jnp.mean(v * v, axis=-1, keepdims=True) + NORM_EPS) * g


def _ada_kernel(c_ref, w_ref, b_ref, o_ref):
    c = c_ref[...]
    a = (c * jax.nn.sigmoid(c)).astype(BF16)
    o_ref[0] = jnp.dot(a, w_ref[0].astype(BF16), preferred_element_type=F32) + b_ref[0]


def _ada(c, w_ada, b_ada, tn=1024):
    depth, d, n = w_ada.shape
    m = c.shape[0]
    return pl.pallas_call(
        _ada_kernel,
        grid=(depth, n // tn),
        in_specs=[pl.BlockSpec((m, d), lambda l, j: (0, 0)),
                  pl.BlockSpec((1, d, tn), lambda l, j: (l, 0, j)),
                  pl.BlockSpec((1, 1, tn), lambda l, j: (l, 0, j))],
        out_specs=pl.BlockSpec((1, m, tn), lambda l, j: (l, 0, j)),
        out_shape=jax.ShapeDtypeStruct((depth, m, n), F32),
        compiler_params=_params("arbitrary", "arbitrary"),
        name="ada_mods",
    )(c, w_ada, b_ada.reshape(depth, 1, n))


def _norm_kernel(x_ref, g_ref, sc_ref, sh_ref, o_ref):
    y = _rms(x_ref[...], g_ref[0]) * (1.0 + sc_ref[0]) + sh_ref[0]
    o_ref[...] = y.astype(o_ref.dtype)


def _norm(x, g3, layer, sc, sh, groups, tr=512):
    m, d = x.shape
    tr = min(tr, m)
    return pl.pallas_call(
        _norm_kernel,
        grid=(m // tr,),
        in_specs=[pl.BlockSpec((tr, d), lambda i: (i, 0)),
                  pl.BlockSpec((1, 1, d), lambda i: (layer, 0, 0)),
                  _mod_spec(sc, tr, m // groups), _mod_spec(sh, tr, m // groups)],
        out_specs=pl.BlockSpec((tr, d), lambda i: (i, 0)),
        out_shape=jax.ShapeDtypeStruct((m, d), BF16),
        compiler_params=_params("arbitrary"),
        name="norm_mod",
    )(x, g3, sc, sh)


def _mm_kernel(*refs, epilogue, nk, nexts, scale, emit_w):
    a_ref, w_ref = refs[:2]
    pos = 2
    next_in = []
    if epilogue == "resid":
        x_ref, gt_ref, g_ref = refs[2:5]
        pos = 5
        for modulated in nexts:
            n_in = 3 if modulated else 1
            next_in.append(refs[pos:pos + n_in])
            pos += n_in
    o_ref = refs[pos]
    next_out = refs[pos + 1:pos + 1 + len(nexts)]
    pos += 1 + len(nexts)
    wb_ref = refs[pos] if emit_w else None
    acc_ref = refs[pos + int(emit_w)] if nk > 1 else None

    def prod():
        if not emit_w:
            return jnp.dot(a_ref[...], w_ref[0], preferred_element_type=F32)
        wb = w_ref[0].astype(BF16)
        wb_ref[0] = wb
        return jnp.dot(a_ref[...], wb, preferred_element_type=F32)

    def finish(acc):
        if epilogue == "relu2":
            r = jnp.maximum(acc, 0.0)
            acc = r * r
        elif epilogue == "scale":
            acc = acc * scale
        elif epilogue == "resid":
            acc = x_ref[...] + gt_ref[0] * _rms(acc, g_ref[0])
            for ins, out in zip(next_in, next_out):
                y = _rms(acc, ins[0][0])
                if len(ins) == 3:
                    y = y * (1.0 + ins[1][0]) + ins[2][0]
                out[...] = y.astype(out.dtype)
        o_ref[...] = acc.astype(o_ref.dtype)

    if nk == 1:
        finish(prod())
        return
    k = pl.program_id(2)

    @pl.when(k == 0)
    def _():
        acc_ref[...] = prod()

    @pl.when(jnp.logical_and(k > 0, k < nk - 1))
    def _():
        acc_ref[...] += prod()

    @pl.when(k == nk - 1)
    def _():
        finish(acc_ref[...] + prod())


def _mm(a, w3, layer, *, out_dtype, n_out=None, col0=0, epilogue="none", scale=1.0, resid=None, nexts=(),
        groups=1, tm=1024, tn=1024, tk=2048):
    m, kdim = a.shape
    n = w3.shape[2] if n_out is None else n_out
    emit_w = w3.dtype == F32
    tm, tn, tk = min(tm, m), min(tn, n), min(tk, kdim)
    if epilogue == "resid":
        tm, tn = min(tm, 512), n
    if emit_w:
        assert tm == m
        tk = min(tk, 1024)
    nk = kdim // tk
    cb = col0 // tn
    rpg = m // groups
    in_specs = [pl.BlockSpec((tm, tk), lambda i, j, k: (i, k)),
                pl.BlockSpec((1, tk, tn), lambda i, j, k: (layer, k, j + cb))]
    args = [a, w3]
    out_specs = [pl.BlockSpec((tm, tn), lambda i, j, k: (i, j))]
    out_shape = [jax.ShapeDtypeStruct((m, n), out_dtype)]
    if epilogue == "resid":
        x, gt, g3, glayer = resid
        in_specs += [pl.BlockSpec((tm, n), lambda i, j, k: (i, 0)), _mod_spec(gt, tm, rpg),
                     pl.BlockSpec((1, 1, n), lambda i, j, k: (glayer, 0, 0))]
        args += [x, gt, g3]
        for ng3, nlayer, sc, sh in nexts:
            in_specs.append(pl.BlockSpec((1, 1, n), lambda i, j, k, nlayer=nlayer: (nlayer, 0, 0)))
            args.append(ng3)
            if sc is not None:
                in_specs += [_mod_spec(sc, tm, rpg), _mod_spec(sh, tm, rpg)]
                args += [sc, sh]
            out_specs.append(pl.BlockSpec((tm, n), lambda i, j, k: (i, 0)))
            out_shape.append(jax.ShapeDtypeStruct((m, n), BF16))
    if emit_w:
        out_specs.append(pl.BlockSpec((1, tk, tn), lambda i, j, k: (0, k, j)))
        out_shape.append(jax.ShapeDtypeStruct((1, kdim, n), BF16))
    return pl.pallas_call(
        functools.partial(_mm_kernel, epilogue=epilogue, nk=nk, scale=scale, emit_w=emit_w,
                          nexts=tuple(sc is not None for _, _, sc, _ in nexts)),
        grid=(m // tm, n // tn, nk),
        in_specs=in_specs,
        out_specs=out_specs,
        out_shape=out_shape,
        scratch_shapes=[pltpu.VMEM((tm, tn), F32)] if nk > 1 else [],
        compiler_params=_params("arbitrary", "arbitrary", "arbitrary"),
        name="mm_" + epilogue,
    )(*args)


def _ret_kernel(lg_ref, q_ref, k_ref, v_ref, g_ref, cos_ref, sin_ref, gng_ref, s0_ref,
                y_ref, sfin_ref, s_scr, *, chunk, n_sub, valid, dk, mxu_dtype):
    h = pl.program_id(1)
    c = pl.program_id(2)

    @pl.when(c == 0)
    def _():
        s_scr[...] = s0_ref[0, 0]

    lg = lg_ref[h]
    row = lax.broadcasted_iota(jnp.int32, (chunk, 1), 0).astype(F32)
    col = lax.broadcasted_iota(jnp.int32, (1, chunk), 1).astype(F32)
    diff = row - col
    causal = diff >= 0.0
    decay = jnp.where(causal, jnp.exp(jnp.where(causal, diff, 0.0) * lg), 0.0)
    q_scale = jnp.exp((row + 1.0) * lg)
    k_scale = jnp.where(row < valid, jnp.exp((valid - 1.0 - row) * lg), 0.0)
    s_scale = jnp.exp(jnp.full((1, s_scr.shape[1]), valid, F32) * lg)
    half = dk // 2

    def rot(x, cos, sin):
        x1, x2 = x[:, :half], x[:, half:]
        return jnp.concatenate([x1 * cos - x2 * sin, x1 * sin + x2 * cos], axis=1)

    for sub in range(n_sub):
        rows = pl.ds(sub * chunk, chunk)
        cos, sin = cos_ref[rows, :], sin_ref[rows, :]
        q = rot(q_ref[0, rows, :].astype(F32), cos, sin)
        k = rot(k_ref[0, rows, :].astype(F32), cos, sin) * (dk ** -0.5)
        v = v_ref[0, rows, :].astype(mxu_dtype)
        s = s_scr[...]
        scores = lax.dot_general(q.astype(mxu_dtype), k.astype(mxu_dtype), NT_DIMS,
                                 preferred_element_type=F32) * decay
        o = jnp.dot(scores.astype(mxu_dtype), v, preferred_element_type=F32)
        o = o + jnp.dot((q * q_scale).astype(mxu_dtype), s.astype(mxu_dtype), preferred_element_type=F32)
        kd = (k * k_scale).astype(mxu_dtype)
        s_scr[...] = s_scale * s + lax.dot_general(kd, v, (((0,), (0,)), ((), ())),
                                                   preferred_element_type=F32)
        mu = jnp.mean(o, axis=-1, keepdims=True)
        ctr = o - mu
        var = jnp.mean(ctr * ctr, axis=-1, keepdims=True)
        on = ctr * lax.rsqrt(var + GN_EPS) * gng_ref[0]
        g = g_ref[0, rows, :].astype(F32)
        y_ref[0, rows, :] = (g * jax.nn.sigmoid(g) * on).astype(y_ref.dtype)

    @pl.when(c == pl.num_programs(2) - 1)
    def _():
        sfin_ref[0, 0] = s_scr[...]


def _retention(proj, cos, sin, s0, gn_g3, log_g, *, batch, chunk, valid, n_sub):
    m, six_d = proj.shape
    d = six_d // 6
    t = m // batch
    dk, dv = d // RET_HEADS, 2 * d // RET_HEADS
    rows = chunk * n_sub
    proj3 = proj.reshape(batch, t, six_d)
    mxu_dtype = BF16 if chunk % 16 == 0 else F32
    kern = functools.partial(_ret_kernel, chunk=chunk, n_sub=n_sub, valid=float(valid), dk=dk,
                             mxu_dtype=mxu_dtype)
    y, s_fin = pl.pallas_call(
        kern,
        grid=(batch, RET_HEADS, t // rows),
        in_specs=[pl.BlockSpec(memory_space=pltpu.SMEM),
                  pl.BlockSpec((1, rows, dk), lambda b, h, c: (b, c, h)),
                  pl.BlockSpec((1, rows, dk), lambda b, h, c: (b, c, RET_HEADS + h)),
                  pl.BlockSpec((1, rows, dv), lambda b, h, c: (b, c, RET_HEADS + h)),
                  pl.BlockSpec((1, rows, dv), lambda b, h, c: (b, c, 2 * RET_HEADS + h)),
                  pl.BlockSpec((rows, dk // 2), lambda b, h, c: (c, 0)),
                  pl.BlockSpec((rows, dk // 2), lambda b, h, c: (c, 0)),
                  pl.BlockSpec((1, 1, dv), lambda b, h, c: (0, 0, h)),
                  pl.BlockSpec((1, 1, dk, dv), lambda b, h, c: (b, h, 0, 0))],
        out_specs=[pl.BlockSpec((1, rows, dv), lambda b, h, c: (b, c, h)),
                   pl.BlockSpec((1, 1, dk, dv), lambda b, h, c: (b, h, 0, 0))],
        out_shape=[jax.ShapeDtypeStruct((batch, t, 2 * d), mxu_dtype),
                   jax.ShapeDtypeStruct((batch, RET_HEADS, dk, dv), F32)],
        scratch_shapes=[pltpu.VMEM((dk, dv), F32)],
        compiler_params=_params("arbitrary", "arbitrary", "arbitrary"),
        name="retention",
    )(log_g, proj3, proj3, proj3, proj3, cos, sin, gn_g3, s0)
    return y.reshape(m, 2 * d).astype(BF16), s_fin


def _suffix_matrix():
    j = np.arange(LANES)[:, None]
    c = np.arange(2 * LANES)[None, :]
    return jnp.asarray(np.where(c < LANES, j > c, True), dtype=BF16)


def _sb_logs(z):
    m = jnp.minimum(z, 0.0)
    mz = m - z
    t = jnp.log(1.0 + jnp.exp(m + mz))
    return m - t, mz - t


def _mask_top(x, mask):
    top = jnp.where(mask, x[:LANES], 0.0)
    return top if x.shape[0] == LANES else jnp.concatenate([top, x[LANES:]], axis=0)


def _sb_suffix(log_keep, u):
    r = jnp.dot(log_keep.astype(BF16), u, preferred_element_type=F32)
    return r[:, :LANES], r[:, LANES:]


def _sb_prompt_kernel(bias_ref, u_ref, q_ref, k_ref, v_ref, o_ref, o_scr, c_scr, *, tile, hps):
    hg = pl.program_id(1)
    qi = pl.program_id(2)
    u = u_ref[...]
    nb = tile // LANES
    lane = lax.broadcasted_iota(jnp.int32, (1, LANES), 1)
    q_ext = jnp.broadcast_to(jnp.where(lane < 3, 1.0, 0.0), (tile, LANES)).astype(BF16)
    tri = (lax.broadcasted_iota(jnp.int32, (LANES, LANES), 1)
           < lax.broadcasted_iota(jnp.int32, (LANES, LANES), 0))

    def bias_columns(hh):
        b0 = jnp.full((1, LANES), bias_ref[hg * hps + hh], F32)
        b1 = b0.astype(BF16).astype(F32)
        b2 = (b0 - b1).astype(BF16).astype(F32)
        b3 = (b0 - b1) - b2
        row = jnp.where(lane == 0, b1, jnp.where(lane == 1, b2, jnp.where(lane == 2, b3, 0.0)))
        return jnp.broadcast_to(row, (tile, LANES)).astype(BF16)

    k_ext = [bias_columns(hh) for hh in range(hps)]

    def logits(hh, k0):
        cols = slice(hh * LANES, (hh + 1) * LANES)
        q_aug = jnp.concatenate([q_ref[0, :, cols], q_ext], axis=1)
        k_aug = jnp.concatenate([k_ref[0, pl.ds(k0, tile), cols].astype(BF16), k_ext[hh]], axis=1)
        return lax.dot_general(q_aug, k_aug, NT_DIMS, preferred_element_type=F32)

    def values(hh, k0):
        return v_ref[0, pl.ds(k0, tile), hh * LANES:(hh + 1) * LANES].astype(BF16)

    kd = pl.multiple_of(qi * tile, tile)
    for hh in range(hps):
        z = logits(hh, kd)
        carry = jnp.zeros((tile, LANES), F32)
        wd = []
        for kb in reversed(range(nb)):
            r0 = kb * LANES
            log_beta, log_keep = _sb_logs(z[r0:, r0:r0 + LANES])
            suffix, total = _sb_suffix(_mask_top(log_keep, tri), u)
            c = carry[r0:]
            w = _mask_top(jnp.exp(log_beta + suffix + c), tri).astype(BF16)
            wd.insert(0, w if kb == 0 else jnp.concatenate([jnp.zeros((r0, LANES), BF16), w], axis=0))
            carry = c + total if kb == 0 else jnp.concatenate([carry[:r0], c + total], axis=0)
        c_scr[hh] = carry
        o_scr[hh] = jnp.dot(jnp.concatenate(wd, axis=1), values(hh, kd), preferred_element_type=F32)

    def left_tile(i, _):
        k0 = pl.multiple_of((qi - 1 - i) * tile, tile)
        for hh in range(hps):
            z = logits(hh, k0)
            carry = c_scr[hh]
            ws = [None] * nb
            for kb in reversed(range(nb)):
                log_beta, log_keep = _sb_logs(z[:, kb * LANES:(kb + 1) * LANES])
                suffix, total = _sb_suffix(log_keep, u)
                ws[kb] = jnp.exp(log_beta + suffix + carry).astype(BF16)
                carry = carry + total
            c_scr[hh] = carry
            o_scr[hh] += jnp.dot(jnp.concatenate(ws, axis=1), values(hh, k0), preferred_element_type=F32)
        return 0

    lax.fori_loop(0, qi, left_tile, 0)
    for hh in range(hps):
        o_ref[0, :, hh * LANES:(hh + 1) * LANES] = o_scr[hh].astype(o_ref.dtype)


def _sb_prompt(q, k, v, bias, *, batch, tile=512, hps=4):
    m, d = q.shape
    t = m // batch
    w = hps * (d // SB_HEADS)
    tile = min(tile, t)
    q3, k3, v3 = (a.reshape(batch, t, d) for a in (q, k, v))
    o = pl.pallas_call(
        functools.partial(_sb_prompt_kernel, tile=tile, hps=hps),
        grid=(batch, SB_HEADS // hps, t // tile),
        in_specs=[pl.BlockSpec(memory_space=pltpu.SMEM),
                  pl.BlockSpec((LANES, 2 * LANES), lambda b, h, i: (0, 0)),
                  pl.BlockSpec((1, tile, w), lambda b, h, i: (b, i, h)),
                  pl.BlockSpec((1, t, w), lambda b, h, i: (b, 0, h)),
                  pl.BlockSpec((1, t, w), lambda b, h, i: (b, 0, h))],
        out_specs=pl.BlockSpec((1, tile, w), lambda b, h, i: (b, i, h)),
        out_shape=jax.ShapeDtypeStruct((batch, t, d), BF16),
        scratch_shapes=[pltpu.VMEM((hps, tile, LANES), F32), pltpu.VMEM((hps, tile, LANES), F32)],
        compiler_params=_params("arbitrary", "arbitrary", "arbitrary"),
        name="sb_prompt",
    )(bias, _suffix_matrix(), q3, k3, v3)
    return o.reshape(m, d)


def _sb_sample_kernel(pt_ref, bias_ref, u_ref, q_ref, kn_ref, vn_ref, *refs, n_q, pps):
    kc_refs, vc_refs = refs[:pps], refs[pps:2 * pps]
    o_ref, c_scr = refs[2 * pps:]
    s = pl.program_id(1)
    rq = SAMPLE_ROWS
    hd = LANES

    @pl.when(s == 0)
    def _():
        o_ref[...] = jnp.zeros_like(o_ref)
        c_scr[...] = jnp.zeros_like(c_scr)

    def head_rows(ref, h):
        return ref[pl.ds(h, PAGE_SIZE, stride=SB_HEADS), :].astype(BF16)

    def page(k_ref, v_ref, masked):
        zs = []
        for h in range(SB_HEADS):
            qh = q_ref[0, :, h * hd:(h + 1) * hd].astype(BF16)
            zh = lax.dot_general(qh, head_rows(k_ref, h), NT_DIMS, preferred_element_type=F32)
            zs.append(zh + bias_ref[h])
        z = jnp.concatenate(zs, axis=0)
        log_beta, log_keep = _sb_logs(z)
        if masked:
            t_q = lax.broadcasted_iota(jnp.int32, z.shape, 0) % rq
            mask = lax.broadcasted_iota(jnp.int32, z.shape, 1) < jnp.minimum(t_q, n_q)
            log_keep = jnp.where(mask, log_keep, 0.0)
        suffix, total = _sb_suffix(log_keep, u_ref[...])
        carry = c_scr[...]
        w = jnp.exp(log_beta + suffix + carry)
        if masked:
            w = jnp.where(mask, w, 0.0)
        c_scr[...] = carry + total
        for h in range(SB_HEADS):
            o_ref[0, :, h * hd:(h + 1) * hd] += jnp.dot(w[h * rq:(h + 1) * rq, :].astype(BF16),
                                                       head_rows(v_ref, h), preferred_element_type=F32)

    @pl.when(s == 0)
    def _():
        page(kn_ref, vn_ref, True)

    @pl.when(s > 0)
    def _():
        for p in range(pps):
            page(kc_refs[p], vc_refs[p], False)


def _sb_sample(q, k_new_page, v_new_page, cache_k, cache_v, page_table, bias, *, n_q):
    b, n_pages = page_table.shape
    d = q.shape[-1]
    rows, hd = cache_k.shape[1:]
    pps = next(p for p in (4, 2, 1) if n_pages % p == 0)

    def cache_spec(p):
        def index_map(i, s, pt):
            return (pt[i, n_pages - 1 - (jnp.maximum(s, 1) - 1) * pps - p], 0, 0)
        return pl.BlockSpec((None, rows, hd), index_map)

    new_spec = pl.BlockSpec((None, rows, hd), lambda i, s, pt: (i, 0, 0))
    grid_spec = pltpu.PrefetchScalarGridSpec(
        num_scalar_prefetch=1,
        grid=(b, n_pages // pps + 1),
        in_specs=[pl.BlockSpec(memory_space=pltpu.SMEM),
                  pl.BlockSpec((LANES, 2 * LANES), lambda i, s, pt: (0, 0)),
                  pl.BlockSpec((1, SAMPLE_ROWS, d), lambda i, s, pt: (i, 0, 0)),
                  new_spec, new_spec] + [cache_spec(p) for p in range(pps)] * 2,
        out_specs=pl.BlockSpec((1, SAMPLE_ROWS, d), lambda i, s, pt: (i, 0, 0)),
        scratch_shapes=[pltpu.VMEM((SB_HEADS * SAMPLE_ROWS, LANES), F32)],
    )
    return pl.pallas_call(
        functools.partial(_sb_sample_kernel, n_q=n_q, pps=pps),
        grid_spec=grid_spec,
        out_shape=jax.ShapeDtypeStruct((b, SAMPLE_ROWS, d), F32),
        compiler_params=_params("arbitrary", "arbitrary"),
        name="sb_sample",
    )(page_table, bias, _suffix_matrix(), q, k_new_page, v_new_page,
      *([cache_k] * pps), *([cache_v] * pps))


def _rope_tables(pos, half):
    freqs = ROPE_BASE ** (-jnp.arange(half, dtype=F32) / half)
    ang = pos.astype(F32)[:, None] * freqs[None, :]
    return jnp.cos(ang), jnp.sin(ang)


def _trunk(x, mods, groups, gains, wts, cast, retention_fn, attention_fn, tm, proj_dtype):
    g_pre_mix, g_post_mix, g_pre_ffn, g_post_ffn, kv_norm_g = gains
    d = x.shape[1]
    n_a = wts["a_w_in"].shape[0]
    depth = wts["w_ff1"].shape[0]
    hd = d // SB_HEADS

    def mm(a, name, layer, col0=0, **kw):
        key = (name, layer, col0)
        if key in cast:
            return _mm(a, cast[key], 0, tm=tm, **kw)
        *outs, cast[key] = _mm(a, wts[name], layer, col0=col0, tm=tm, **kw)
        return outs

    states = []
    k_new = v_new = None
    h = _norm(x, g_pre_mix, 0, mods[0][1], mods[0][0], groups)
    hn = None
    for layer in range(depth):
        sh1, sc1, gt1, sh2, sc2, gt2 = mods[layer]
        if layer < n_a:
            proj, = mm(h, "a_w_in", layer, out_dtype=proj_dtype)
            y, s_fin = retention_fn(layer, proj)
            states.append(s_fin)
            w_out, j = "a_w_out", layer
        else:
            if k_new is None:
                k_new, = mm(hn, "w_kv", 0, col0=0, out_dtype=F32, n_out=d)
                v_new, = mm(hn, "w_kv", 0, col0=d, out_dtype=F32, n_out=d)
            j = layer - n_a
            q, = mm(h, "b_w_q", j, out_dtype=BF16, epilogue="scale", scale=hd ** -0.5)
            y = attention_fn(j, q, k_new, v_new)
            w_out = "b_w_out"
        x, h = mm(y, w_out, j, out_dtype=F32, epilogue="resid", resid=(x, gt1, g_post_mix, layer),
                  nexts=((g_pre_ffn, layer, sc2, sh2),), groups=groups)
        f, = mm(h, "w_ff1", layer, out_dtype=BF16, epilogue="relu2")
        nexts = []
        if layer + 1 < depth:
            nexts.append((g_pre_mix, layer + 1, mods[layer + 1][1], mods[layer + 1][0]))
        if layer + 1 == n_a:
            nexts.append((kv_norm_g, 0, None, None))
        outs = mm(f, "w_ff2", layer, out_dtype=F32, epilogue="resid", resid=(x, gt2, g_post_ffn, layer),
                  nexts=tuple(nexts), groups=groups)
        x = outs[0]
        if layer + 1 < depth:
            h = outs[1]
        if layer + 1 == n_a:
            hn = outs[-1]
    return x, jnp.stack(states, axis=0), k_new, v_new


def kernel(x_prompt, x_sample, c_prompt, c_sample, state_ret, cache_k, cache_v, page_table, w_ada, b_ada,
           g_pre_mix, g_post_mix, g_pre_ffn, g_post_ffn, w_ff1, w_ff2, a_w_in, a_w_out, a_gn_g, kv_norm_g,
           w_kv, b_w_q, b_w_out, b_sb):
    bp, seq, d = x_prompt.shape
    bs, dec_seq, _ = x_sample.shape
    depth = w_ada.shape[0]
    n_pool, page, _, _ = cache_k.shape
    n_pages = page_table.shape[1]
    past_len = n_pages * page
    hd = d // SB_HEADS
    dk, dv = d // RET_HEADS, 2 * d // RET_HEADS
    rs = SAMPLE_ROWS

    gains = (g_pre_mix.reshape(depth, 1, d), g_post_mix.reshape(depth, 1, d),
             g_pre_ffn.reshape(depth, 1, d), g_post_ffn.reshape(depth, 1, d), kv_norm_g.reshape(1, 1, d))
    wts = dict(w_ff1=w_ff1, w_ff2=w_ff2, a_w_in=a_w_in, a_w_out=a_w_out, w_kv=w_kv[None], b_w_q=b_w_q,
               b_w_out=b_w_out)
    cast = {}
    gn_g3 = a_gn_g.reshape(a_gn_g.shape[0], 1, 2 * d)
    log_g = jnp.log1p(-jnp.exp2(-5.0 - jnp.arange(RET_HEADS, dtype=F32)))

    n_c = bp + bs
    c_rows = -(-n_c // SUBLANES) * SUBLANES
    c_all = jnp.concatenate([c_prompt, c_sample, jnp.zeros((c_rows - n_c, d), F32)], axis=0)
    mods_all = _ada(c_all, w_ada, b_ada)
    mods_p = [[m[:, None, :] for m in jnp.split(mods_all[l, :bp], 6, axis=-1)] for l in range(depth)]
    mods_s = [[jnp.repeat(m, rs, axis=0)[None] for m in jnp.split(mods_all[l, bp:n_c], 6, axis=-1)]
              for l in range(depth)]

    x_s =jnp.pad(x_sample, ((0, 0), (0, rs - dec_seq), (0, 0))).reshape(bs * rs, d)
    cos_s, sin_s = _rope_tables(past_len + jnp.arange(rs, dtype=jnp.int32), dk // 2)
    cache_k3 = cache_k.reshape(n_pool, page * SB_HEADS, hd)
    cache_v3 = cache_v.reshape(n_pool, page * SB_HEADS, hd)

    def ret_sample(layer, proj):
        return _retention(proj, cos_s, sin_s, state_ret[layer], gn_g3[layer:layer + 1], log_g, batch=bs,
                          chunk=rs, valid=dec_seq, n_sub=1)

    def attn_sample(j, q, k_new, v_new):
        pad = ((0, 0), (0, (page - rs) * SB_HEADS), (0, 0))
        kn = jnp.pad(k_new.reshape(bs, rs * SB_HEADS, hd), pad)
        vn = jnp.pad(v_new.reshape(bs, rs * SB_HEADS, hd), pad)
        o = _sb_sample(q.astype(F32).reshape(bs, rs, d), kn, vn, cache_k3, cache_v3, page_table, b_sb[j],
                       n_q=dec_seq)
        return o.reshape(bs * rs, d).astype(BF16)

    y_s, st_s, k_s, v_s = _trunk(x_s, mods_s, 1, gains, wts, cast, ret_sample, attn_sample, bs * rs, F32)

    cos_p, sin_p = _rope_tables(jnp.arange(seq, dtype=jnp.int32), dk // 2)
    s0_p = jnp.zeros((bp, RET_HEADS, dk, dv), F32)

    def ret_prompt(layer, proj):
        return _retention(proj, cos_p, sin_p, s0_p, gn_g3[layer:layer + 1], log_g, batch=bp,
                          chunk=RET_CHUNK, valid=RET_CHUNK, n_sub=4)

    def attn_prompt(j, q, k_new, v_new):
        return _sb_prompt(q, k_new, v_new, b_sb[j], batch=bp)

    y_p, st_p, k_p, v_p = _trunk(x_prompt.reshape(bp * seq, d), mods_p, bp, gains, wts, cast, ret_prompt,
                                 attn_prompt, 1024, BF16)

    def unpad(a, *tail):
        return a.reshape(bs, rs, *tail)[:, :dec_seq]

    return (y_p.reshape(bp, seq, d), unpad(y_s, d), st_p, st_s,
            k_p.reshape(bp, seq, SB_HEADS, hd), v_p.reshape(bp, seq, SB_HEADS, hd),
            unpad(k_s, SB_HEADS, hd), unpad(v_s, SB_HEADS, hd))
```

```python
import functools

import jax
import jax.numpy as jnp
import numpy as np
from jax import lax
from jax.experimental import pallas as pl
from jax.experimental.pallas import tpu as pltpu

F32 = jnp.float32
BF16 = jnp.bfloat16

LANES = 128
SUBLANES = 8
VMEM_LIMIT_BYTES = 56 * 1024 * 1024

RET_HEADS = 8
RET_CHUNK = 128
ROPE_BASE = 10000.0
SB_HEADS = 16
PAGE_SIZE = 128
NORM_EPS = 1e-6
GN_EPS = 1e-6
SAMPLE_ROWS = SUBLANES
HEAD_PITCH = SB_HEADS + SUBLANES
NT_DIMS = (((1,), (1,)), ((), ()))


def _params(*sem):
    return pltpu.CompilerParams(dimension_semantics=sem, vmem_limit_bytes=VMEM_LIMIT_BYTES)


def _mod_spec(mod, tile_rows, rows_per_group):
    d = mod.shape[-1]
    if mod.shape[1] == 1:
        return pl.BlockSpec((1, 1, d), lambda *g: ((g[0] * tile_rows) // rows_per_group, 0, 0))
    return pl.BlockSpec((1, tile_rows, d), lambda *g: (0, g[0], 0))


def _rms(v, g):
    return v * lax.rsqrt(jnp.mean(v * v, axis=-1, keepdims=True) + NORM_EPS) * g


def _ada_kernel(c_ref, w_ref, b_ref, o_ref):
    c = c_ref[...]
    a = (c * jax.nn.sigmoid(c)).astype(BF16)
    o_ref[0] = jnp.dot(a, w_ref[0].astype(BF16), preferred_element_type=F32) + b_ref[0]


def _ada(c, w_ada, b_ada, tn=1024):
    depth, d, n = w_ada.shape
    m = c.shape[0]
    return pl.pallas_call(
        _ada_kernel,
        grid=(depth, n // tn),
        in_specs=[pl.BlockSpec((m, d), lambda l, j: (0, 0)),
                  pl.BlockSpec((1, d, tn), lambda l, j: (l, 0, j)),
                  pl.BlockSpec((1, 1, tn), lambda l, j: (l, 0, j))],
        out_specs=pl.BlockSpec((1, m, tn), lambda l, j: (l, 0, j)),
        out_shape=jax.ShapeDtypeStruct((depth, m, n), F32),
        compiler_params=_params("arbitrary", "arbitrary"),
        name="ada_mods",
    )(c, w_ada, b_ada.reshape(depth, 1, n))


def _norm_kernel(x_ref, g_ref, sc_ref, sh_ref, o_ref):
    y = _rms(x_ref[...], g_ref[0]) * (1.0 + sc_ref[0]) + sh_ref[0]
    o_ref[...] = y.astype(o_ref.dtype)


def _norm(x, g3, layer, sc, sh, groups, tr=512):
    m, d = x.shape
    tr = min(tr, m)
    return pl.pallas_call(
        _norm_kernel,
        grid=(m // tr,),
        in_specs=[pl.BlockSpec((tr, d), lambda i: (i, 0)),
                  pl.BlockSpec((1, 1, d), lambda i: (layer, 0, 0)),
                  _mod_spec(sc, tr, m // groups), _mod_spec(sh, tr, m // groups)],
        out_specs=pl.BlockSpec((tr, d), lambda i: (i, 0)),
        out_shape=jax.ShapeDtypeStruct((m, d), BF16),
        compiler_params=_params("arbitrary"),
        name="norm_mod",
    )(x, g3, sc, sh)


def _mm_kernel(*refs, epilogue, nk, nexts, scale, emit_w):
    a_ref, w_ref = refs[:2]
    pos = 2
    next_in = []
    if epilogue == "resid":
        x_ref, gt_ref, g_ref = refs[2:5]
        pos = 5
        for modulated in nexts:
            n_in = 3 if modulated else 1
            next_in.append(refs[pos:pos + n_in])
            pos += n_in
    o_ref = refs[pos]
    next_out = refs[pos + 1:pos + 1 + len(nexts)]
    pos += 1 + len(nexts)
    wb_ref = refs[pos] if emit_w else None
    acc_ref = refs[pos + int(emit_w)] if nk > 1 else None

    def prod():
        if not emit_w:
            return jnp.dot(a_ref[...], w_ref[0], preferred_element_type=F32)
        wb = w_ref[0].astype(BF16)
        wb_ref[0] = wb
        return jnp.dot(a_ref[...], wb, preferred_element_type=F32)

    def finish(acc):
        if epilogue == "relu2":
            r = jnp.maximum(acc, 0.0)
            acc = r * r
        elif epilogue == "scale":
            acc = acc * scale
        elif epilogue == "resid":
            acc = x_ref[...] + gt_ref[0] * _rms(acc, g_ref[0])
            for ins, out in zip(next_in, next_out):
                y = _rms(acc, ins[0][0])
                if len(ins) == 3:
                    y = y * (1.0 + ins[1][0]) + ins[2][0]
                out[...] = y.astype(out.dtype)
        o_ref[...] = acc.astype(o_ref.dtype)

    if nk == 1:
        finish(prod())
        return
    k = pl.program_id(2)

    @pl.when(k == 0)
    def _():
        acc_ref[...] = prod()

    @pl.when(jnp.logical_and(k > 0, k < nk - 1))
    def _():
        acc_ref[...] += prod()

    @pl.when(k == nk - 1)
    def _():
        finish(acc_ref[...] + prod())


def _mm(a, w3, layer, *, out_dtype, n_out=None, col0=0, epilogue="none", scale=1.0, resid=None, nexts=(),
        groups=1, tm=1024, tn=1024, tk=2048):
    m, kdim = a.shape
    n = w3.shape[2] if n_out is None else n_out
    emit_w = w3.dtype == F32
    tm, tn, tk = min(tm, m), min(tn, n), min(tk, kdim)
    if epilogue == "resid":
        tm, tn = min(tm, 512), n
    if emit_w:
        assert tm == m
        tk = min(tk, 1024)
    nk = kdim // tk
    cb = col0 // tn
    rpg = m // groups
    in_specs = [pl.BlockSpec((tm, tk), lambda i, j, k: (i, k)),
                pl.BlockSpec((1, tk, tn), lambda i, j, k: (layer, k, j + cb))]
    args = [a, w3]
    out_specs = [pl.BlockSpec((tm, tn), lambda i, j, k: (i, j))]
    out_shape = [jax.ShapeDtypeStruct((m, n), out_dtype)]
    if epilogue == "resid":
        x, gt, g3, glayer = resid
        in_specs += [pl.BlockSpec((tm, n), lambda i, j, k: (i, 0)), _mod_spec(gt, tm, rpg),
                     pl.BlockSpec((1, 1, n), lambda i, j, k: (glayer, 0, 0))]
        args += [x, gt, g3]
        for ng3, nlayer, sc, sh in nexts:
            in_specs.append(pl.BlockSpec((1, 1, n), lambda i, j, k, nlayer=nlayer: (nlayer, 0, 0)))
            args.append(ng3)
            if sc is not None:
                in_specs += [_mod_spec(sc, tm, rpg), _mod_spec(sh, tm, rpg)]
                args += [sc, sh]
            out_specs.append(pl.BlockSpec((tm, n), lambda i, j, k: (i, 0)))
            out_shape.append(jax.ShapeDtypeStruct((m, n), BF16))
    if emit_w:
        out_specs.append(pl.BlockSpec((1, tk, tn), lambda i, j, k: (0, k, j)))
        out_shape.append(jax.ShapeDtypeStruct((1, kdim, n), BF16))
    return pl.pallas_call(
        functools.partial(_mm_kernel, epilogue=epilogue, nk=nk, scale=scale, emit_w=emit_w,
                          nexts=tuple(sc is not None for _, _, sc, _ in nexts)),
        grid=(m // tm, n // tn, nk),
        in_specs=in_specs,
        out_specs=out_specs,
        out_shape=out_shape,
        scratch_shapes=[pltpu.VMEM((tm, tn), F32)] if nk > 1 else [],
        compiler_params=_params("arbitrary", "arbitrary", "arbitrary"),
        name="mm_" + epilogue,
    )(*args)


def _ret_kernel(lg_ref, q_ref, k_ref, v_ref, g_ref, cos_ref, sin_ref, gng_ref, s0_ref,
                y_ref, sfin_ref, s_scr, *, chunk, n_sub, valid, dk, mxu_dtype):
    h = pl.program_id(1)
    c = pl.program_id(2)

    @pl.when(c == 0)
    def _():
        s_scr[...] = s0_ref[0, 0]

    lg = lg_ref[h]
    row = lax.broadcasted_iota(jnp.int32, (chunk, 1), 0).astype(F32)
    col = lax.broadcasted_iota(jnp.int32, (1, chunk), 1).astype(F32)
    diff = row - col
    causal = diff >= 0.0
    decay = jnp.where(causal, jnp.exp(jnp.where(causal, diff, 0.0) * lg), 0.0)
    q_scale = jnp.exp((row + 1.0) * lg)
    k_scale = jnp.where(row < valid, jnp.exp((valid - 1.0 - row) * lg), 0.0)
    s_scale = jnp.exp(jnp.full((1, s_scr.shape[1]), valid, F32) * lg)
    half = dk // 2

    def rot(x, cos, sin):
        x1, x2 = x[:, :half], x[:, half:]
        return jnp.concatenate([x1 * cos - x2 * sin, x1 * sin + x2 * cos], axis=1)

    for sub in range(n_sub):
        rows = pl.ds(sub * chunk, chunk)
        cos, sin = cos_ref[rows, :], sin_ref[rows, :]
        q = rot(q_ref[0, rows, :].astype(F32), cos, sin)
        k = rot(k_ref[0, rows, :].astype(F32), cos, sin) * (dk ** -0.5)
        v = v_ref[0, rows, :].astype(mxu_dtype)
        s = s_scr[...]
        scores = lax.dot_general(q.astype(mxu_dtype), k.astype(mxu_dtype), NT_DIMS,
                                 preferred_element_type=F32) * decay
        o = jnp.dot(scores.astype(mxu_dtype), v, preferred_element_type=F32)
        o = o + jnp.dot((q * q_scale).astype(mxu_dtype), s.astype(mxu_dtype), preferred_element_type=F32)
        kd = (k * k_scale).astype(mxu_dtype)
        s_scr[...] = s_scale * s + lax.dot_general(kd, v, (((0,), (0,)), ((), ())),
                                                   preferred_element_type=F32)
        mu = jnp.mean(o, axis=-1, keepdims=True)
        ctr = o - mu
        var = jnp.mean(ctr * ctr, axis=-1, keepdims=True)
        on = ctr * lax.rsqrt(var + GN_EPS) * gng_ref[0]
        g = g_ref[0, rows, :].astype(F32)
        y_ref[0, rows, :] = (g * jax.nn.sigmoid(g) * on).astype(y_ref.dtype)

    @pl.when(c == pl.num_programs(2) - 1)
    def _():
        sfin_ref[0, 0] = s_scr[...]


def _retention(proj, cos, sin, s0, gn_g3, log_g, *, batch, chunk, valid, n_sub):
    m, six_d = proj.shape
    d = six_d // 6
    t = m // batch
    dk, dv = d // RET_HEADS, 2 * d // RET_HEADS
    rows = chunk * n_sub
    proj3 = proj.reshape(batch, t, six_d)
    mxu_dtype = BF16 if chunk % 16 == 0 else F32
    kern = functools.partial(_ret_kernel, chunk=chunk, n_sub=n_sub, valid=float(valid), dk=dk,
                             mxu_dtype=mxu_dtype)
    y, s_fin = pl.pallas_call(
        kern,
        grid=(batch, RET_HEADS, t // rows),
        in_specs=[pl.BlockSpec(memory_space=pltpu.SMEM),
                  pl.BlockSpec((1, rows, dk), lambda b, h, c: (b, c, h)),
                  pl.BlockSpec((1, rows, dk), lambda b, h, c: (b, c, RET_HEADS + h)),
                  pl.BlockSpec((1, rows, dv), lambda b, h, c: (b, c, RET_HEADS + h)),
                  pl.BlockSpec((1, rows, dv), lambda b, h, c: (b, c, 2 * RET_HEADS + h)),
                  pl.BlockSpec((rows, dk // 2), lambda b, h, c: (c, 0)),
                  pl.BlockSpec((rows, dk // 2), lambda b, h, c: (c, 0)),
                  pl.BlockSpec((1, 1, dv), lambda b, h, c: (0, 0, h)),
                  pl.BlockSpec((1, 1, dk, dv), lambda b, h, c: (b, h, 0, 0))],
        out_specs=[pl.BlockSpec((1, rows, dv), lambda b, h, c: (b, c, h)),
                   pl.BlockSpec((1, 1, dk, dv), lambda b, h, c: (b, h, 0, 0))],
        out_shape=[jax.ShapeDtypeStruct((batch, t, 2 * d), mxu_dtype),
                   jax.ShapeDtypeStruct((batch, RET_HEADS, dk, dv), F32)],
        scratch_shapes=[pltpu.VMEM((dk, dv), F32)],
        compiler_params=_params("arbitrary", "arbitrary", "arbitrary"),
        name="retention",
    )(log_g, proj3, proj3, proj3, proj3, cos, sin, gn_g3, s0)
    return y.reshape(m, 2 * d).astype(BF16), s_fin


def _suffix_matrix():
    j = np.arange(LANES)[:, None]
    c = np.arange(2 * LANES)[None, :]
    return jnp.asarray(np.where(c < LANES, j > c, True), dtype=BF16)


def _sb_logs(z):
    m = jnp.minimum(z, 0.0)
    mz = m - z
    t = jnp.log(1.0 + jnp.exp(m + mz))
    return m - t, mz - t


def _mask_top(x, mask):
    top = jnp.where(mask, x[:LANES], 0.0)
    return top if x.shape[0] == LANES else jnp.concatenate([top, x[LANES:]], axis=0)


def _sb_suffix(log_keep, u):
    r = jnp.dot(log_keep.astype(BF16), u, preferred_element_type=F32)
    return r[:, :LANES], r[:, LANES:]


def _sb_prompt_kernel(bias_ref, u_ref, q_ref, k_ref, v_ref, o_ref, o_scr, c_scr, *, tile, hps):
    hg = pl.program_id(1)
    qi = pl.program_id(2)
    u = u_ref[...]
    nb = tile // LANES
    lane = lax.broadcasted_iota(jnp.int32, (1, LANES), 1)
    q_ext = jnp.broadcast_to(jnp.where(lane < 3, 1.0, 0.0), (tile, LANES)).astype(BF16)
    tri = (lax.broadcasted_iota(jnp.int32, (LANES, LANES), 1)
           < lax.broadcasted_iota(jnp.int32, (LANES, LANES), 0))

    def bias_columns(hh):
        b0 = jnp.full((1, LANES), bias_ref[hg * hps + hh], F32)
        b1 = b0.astype(BF16).astype(F32)
        b2 = (b0 - b1).astype(BF16).astype(F32)
        b3 = (b0 - b1) - b2
        row = jnp.where(lane == 0, b1, jnp.where(lane == 1, b2, jnp.where(lane == 2, b3, 0.0)))
        return jnp.broadcast_to(row, (tile, LANES)).astype(BF16)

    k_ext = [bias_columns(hh) for hh in range(hps)]

    def logits(hh, k0):
        cols = slice(hh * LANES, (hh + 1) * LANES)
        q_aug = jnp.concatenate([q_ref[0, :, cols], q_ext], axis=1)
        k_aug = jnp.concatenate([k_ref[0, pl.ds(k0, tile), cols].astype(BF16), k_ext[hh]], axis=1)
        return lax.dot_general(q_aug, k_aug, NT_DIMS, preferred_element_type=F32)

    def values(hh, k0):
        return v_ref[0, pl.ds(k0, tile), hh * LANES:(hh + 1) * LANES].astype(BF16)

    kd = pl.multiple_of(qi * tile, tile)
    for hh in range(hps):
        z = logits(hh, kd)
        carry = jnp.zeros((tile, LANES), F32)
        wd = []
        for kb in reversed(range(nb)):
            r0 = kb * LANES
            log_beta, log_keep = _sb_logs(z[r0:, r0:r0 + LANES])
            suffix, total = _sb_suffix(_mask_top(log_keep, tri), u)
            c = carry[r0:]
            w = _mask_top(jnp.exp(log_beta + suffix + c), tri).astype(BF16)
            wd.insert(0, w if kb == 0 else jnp.concatenate([jnp.zeros((r0, LANES), BF16), w], axis=0))
            carry = c + total if kb == 0 else jnp.concatenate([carry[:r0], c + total], axis=0)
        c_scr[hh] = carry
        o_scr[hh] = jnp.dot(jnp.concatenate(wd, axis=1), values(hh, kd), preferred_element_type=F32)

    def left_tile(i, _):
        k0 = pl.multiple_of((qi - 1 - i) * tile, tile)
        for hh in range(hps):
            z = logits(hh, k0)
            carry = c_scr[hh]
            ws = [None] * nb
            for kb in reversed(range(nb)):
                log_beta, log_keep = _sb_logs(z[:, kb * LANES:(kb + 1) * LANES])
                suffix, total = _sb_suffix(log_keep, u)
                ws[kb] = jnp.exp(log_beta + suffix + carry).astype(BF16)
                carry = carry + total
            c_scr[hh] = carry
            o_scr[hh] += jnp.dot(jnp.concatenate(ws, axis=1), values(hh, k0), preferred_element_type=F32)
        return 0

    lax.fori_loop(0, qi, left_tile, 0)
    for hh in range(hps):
        o_ref[0, :, hh * LANES:(hh + 1) * LANES] = o_scr[hh].astype(o_ref.dtype)


def _sb_prompt(q, k, v, bias, *, batch, tile=512, hps=4):
    m, d = q.shape
    t = m // batch
    w = hps * (d // SB_HEADS)
    tile = min(tile, t)
    q3, k3, v3 = (a.reshape(batch, t, d) for a in (q, k, v))
    o = pl.pallas_call(
        functools.partial(_sb_prompt_kernel, tile=tile, hps=hps),
        grid=(batch, SB_HEADS // hps, t // tile),
        in_specs=[pl.BlockSpec(memory_space=pltpu.SMEM),
                  pl.BlockSpec((LANES, 2 * LANES), lambda b, h, i: (0, 0)),
                  pl.BlockSpec((1, tile, w), lambda b, h, i: (b, i, h)),
                  pl.BlockSpec((1, t, w), lambda b, h, i: (b, 0, h)),
                  pl.BlockSpec((1, t, w), lambda b, h, i: (b, 0, h))],
        out_specs=pl.BlockSpec((1, tile, w), lambda b, h, i: (b, i, h)),
        out_shape=jax.ShapeDtypeStruct((batch, t, d), BF16),
        scratch_shapes=[pltpu.VMEM((hps, tile, LANES), F32), pltpu.VMEM((hps, tile, LANES), F32)],
        compiler_params=_params("arbitrary", "arbitrary", "arbitrary"),
        name="sb_prompt",
    )(bias, _suffix_matrix(), q3, k3, v3)
    return o.reshape(m, d)


def _sb_sample_kernel(pt_ref, bias_ref, u_ref, q_ref, kn_ref, vn_ref, kc_hbm, vc_hbm, o_ref,
                      c_scr, kbuf, vbuf, sem, *, n_q, pps, n_pages):
    b = pl.program_id(0)
    s = pl.program_id(1)
    n_b = pl.num_programs(0)
    last = pl.num_programs(1) - 1
    rq = SAMPLE_ROWS
    hd = LANES

    def page_copies(bb, ss):
        slot = (ss - 1) % 2
        out = []
        for p in range(pps):
            pg = pt_ref[bb, n_pages - 1 - (ss - 1) * pps - p]
            for i, (hbm, buf) in enumerate(((kc_hbm, kbuf), (vc_hbm, vbuf))):
                out.append(pltpu.make_async_copy(hbm.at[pg], buf.at[slot, p, :, pl.ds(0, SB_HEADS), :],
                                                 sem.at[slot, i, p]))
        return out

    @pl.when(jnp.logical_and(b == 0, s == 0))
    def _():
        for c in page_copies(0, 1):
            c.start()

    @pl.when(jnp.logical_and(s >= 1, s < last))
    def _():
        for c in page_copies(b, s + 1):
            c.start()

    @pl.when(jnp.logical_and(s == last, b < n_b - 1))
    def _():
        for c in page_copies(b + 1, 1):
            c.start()

    @pl.when(s == 0)
    def _():
        o_ref[...] = jnp.zeros_like(o_ref)
        c_scr[...] = jnp.zeros_like(c_scr)

    def page(k_rows, v_rows, masked):
        zs = []
        for h in range(SB_HEADS):
            qh = q_ref[0, :, h * hd:(h + 1) * hd].astype(BF16)
            zh = lax.dot_general(qh, k_rows(h), NT_DIMS, preferred_element_type=F32)
            zs.append(zh + bias_ref[h])
        z = jnp.concatenate(zs, axis=0)
        log_beta, log_keep = _sb_logs(z)
        if masked:
            t_q = lax.broadcasted_iota(jnp.int32, z.shape, 0) % rq
            mask = lax.broadcasted_iota(jnp.int32, z.shape, 1) < jnp.minimum(t_q, n_q)
            log_keep = jnp.where(mask, log_keep, 0.0)
        suffix, total = _sb_suffix(log_keep, u_ref[...])
        carry = c_scr[...]
        w = jnp.exp(log_beta + suffix + carry)
        if masked:
            w = jnp.where(mask, w, 0.0)
        c_scr[...] = carry + total
        for h in range(SB_HEADS):
            o_ref[0, :, h * hd:(h + 1) * hd] += jnp.dot(w[h * rq:(h + 1) * rq, :].astype(BF16), v_rows(h),
                                                       preferred_element_type=F32)

    def strided_rows(ref2d, pitch):
        return lambda h: ref2d[pl.ds(h, PAGE_SIZE, stride=pitch), :].astype(BF16)

    @pl.when(s == 0)
    def _():
        page(strided_rows(kn_ref, SB_HEADS), strided_rows(vn_ref, SB_HEADS), True)

    @pl.when(s > 0)
    def _():
        for c in page_copies(b, s):
            c.wait()
        slot = (s - 1) % 2
        for p in range(pps):
            k2d = kbuf.at[slot, p].reshape(PAGE_SIZE * HEAD_PITCH, hd)
            v2d = vbuf.at[slot, p].reshape(PAGE_SIZE * HEAD_PITCH, hd)
            page(strided_rows(k2d, HEAD_PITCH), strided_rows(v2d, HEAD_PITCH), False)


def _sb_sample(q, k_new_page, v_new_page, cache_k, cache_v, page_table, bias, *, n_q):
    b, n_pages = page_table.shape
    d = q.shape[-1]
    rows, hd = k_new_page.shape[1:]
    pps = next(p for p in (4, 2, 1) if n_pages % p == 0)
    assert (n_pages // pps) % 2 == 0

    new_spec = pl.BlockSpec((None, rows, hd), lambda i, s, pt: (i, 0, 0))
    page_buf = pltpu.VMEM((2, pps, PAGE_SIZE, HEAD_PITCH, hd), F32)
    grid_spec = pltpu.PrefetchScalarGridSpec(
        num_scalar_prefetch=1,
        grid=(b, n_pages // pps + 1),
        in_specs=[pl.BlockSpec(memory_space=pltpu.SMEM),
                  pl.BlockSpec((LANES, 2 * LANES), lambda i, s, pt: (0, 0)),
                  pl.BlockSpec((1, SAMPLE_ROWS, d), lambda i, s, pt: (i, 0, 0)),
                  new_spec, new_spec,
                  pl.BlockSpec(memory_space=pl.ANY), pl.BlockSpec(memory_space=pl.ANY)],
        out_specs=pl.BlockSpec((1, SAMPLE_ROWS, d), lambda i, s, pt: (i, 0, 0)),
        scratch_shapes=[pltpu.VMEM((SB_HEADS * SAMPLE_ROWS, LANES), F32), page_buf, page_buf,
                        pltpu.SemaphoreType.DMA((2, 2, pps))],
    )
    return pl.pallas_call(
        functools.partial(_sb_sample_kernel, n_q=n_q, pps=pps, n_pages=n_pages),
        grid_spec=grid_spec,
        out_shape=jax.ShapeDtypeStruct((b, SAMPLE_ROWS, d), F32),
        compiler_params=_params("arbitrary", "arbitrary"),
        name="sb_sample",
    )(page_table, bias, _suffix_matrix(), q, k_new_page, v_new_page, cache_k, cache_v)


def _rope_tables(pos, half):
    freqs = ROPE_BASE ** (-jnp.arange(half, dtype=F32) / half)
    ang = pos.astype(F32)[:, None] * freqs[None, :]
    return jnp.cos(ang), jnp.sin(ang)


def _trunk(x, mods, groups, gains, wts, cast, retention_fn, attention_fn, tm, proj_dtype):
    g_pre_mix, g_post_mix, g_pre_ffn, g_post_ffn, kv_norm_g = gains
    d = x.shape[1]
    n_a = wts["a_w_in"].shape[0]
    depth = wts["w_ff1"].shape[0]
    hd = d // SB_HEADS

    def mm(a, name, layer, col0=0, **kw):
        key = (name, layer, col0)
        if key in cast:
            return _mm(a, cast[key], 0, tm=tm, **kw)
        *outs, cast[key] = _mm(a, wts[name], layer, col0=col0, tm=tm, **kw)
        return outs

    states = []
    k_new = v_new = None
    h = _norm(x, g_pre_mix, 0, mods[0][1], mods[0][0], groups)
    hn = None
    for layer in range(depth):
        sh1, sc1, gt1, sh2, sc2, gt2 = mods[layer]
        if layer < n_a:
            proj, = mm(h, "a_w_in", layer, out_dtype=proj_dtype)
            y, s_fin = retention_fn(layer, proj)
            states.append(s_fin)
            w_out, j = "a_w_out", layer
        else:
            if k_new is None:
                k_new, = mm(hn, "w_kv", 0, col0=0, out_dtype=F32, n_out=d)
                v_new, = mm(hn, "w_kv", 0, col0=d, out_dtype=F32, n_out=d)
            j = layer - n_a
            q, = mm(h, "b_w_q", j, out_dtype=BF16, epilogue="scale", scale=hd ** -0.5)
            y = attention_fn(j, q, k_new, v_new)
            w_out = "b_w_out"
        x, h = mm(y, w_out, j, out_dtype=F32, epilogue="resid", resid=(x, gt1, g_post_mix, layer),
                  nexts=((g_pre_ffn, layer, sc2, sh2),), groups=groups)
        f, = mm(h, "w_ff1", layer, out_dtype=BF16, epilogue="relu2")
        nexts = []
        if layer + 1 < depth:
            nexts.append((g_pre_mix, layer + 1, mods[layer + 1][1], mods[layer + 1][0]))
        if layer + 1 == n_a:
            nexts.append((kv_norm_g, 0, None, None))
        outs = mm(f, "w_ff2", layer, out_dtype=F32, epilogue="resid", resid=(x, gt2, g_post_ffn, layer),
                  nexts=tuple(nexts), groups=groups)
        x = outs[0]
        if layer + 1 < depth:
            h = outs[1]
        if layer + 1 == n_a:
            hn = outs[-1]
    return x, jnp.stack(states, axis=0), k_new, v_new


def kernel(x_prompt, x_sample, c_prompt, c_sample, state_ret, cache_k, cache_v, page_table, w_ada, b_ada,
           g_pre_mix, g_post_mix, g_pre_ffn, g_post_ffn, w_ff1, w_ff2, a_w_in, a_w_out, a_gn_g, kv_norm_g,
           w_kv, b_w_q, b_w_out, b_sb):
    bp, seq, d = x_prompt.shape
    bs, dec_seq, _ = x_sample.shape
    depth = w_ada.shape[0]
    n_pool, page, _, _ = cache_k.shape
    n_pages = page_table.shape[1]
    past_len = n_pages * page
    hd = d // SB_HEADS
    dk, dv = d // RET_HEADS, 2 * d // RET_HEADS
    rs = SAMPLE_ROWS

    gains = (g_pre_mix.reshape(depth, 1, d), g_post_mix.reshape(depth, 1, d),
             g_pre_ffn.reshape(depth, 1, d), g_post_ffn.reshape(depth, 1, d), kv_norm_g.reshape(1, 1, d))
    wts = dict(w_ff1=w_ff1, w_ff2=w_ff2, a_w_in=a_w_in, a_w_out=a_w_out, w_kv=w_kv[None], b_w_q=b_w_q,
               b_w_out=b_w_out)
    cast = {}
    gn_g3 = a_gn_g.reshape(a_gn_g.shape[0], 1, 2 * d)
    log_g = jnp.log1p(-jnp.exp2(-5.0 - jnp.arange(RET_HEADS, dtype=F32)))

    n_c = bp + bs
    c_rows = -(-n_c // SUBLANES) * SUBLANES
    c_all = jnp.concatenate([c_prompt, c_sample, jnp.zeros((c_rows - n_c, d), F32)], axis=0)
    mods_all = _ada(c_all, w_ada, b_ada)
    mods_p = [[m[:, None, :] for m in jnp.split(mods_all[l, :bp], 6, axis=-1)] for l in range(depth)]
    mods_s = [[jnp.repeat(m, rs, axis=0)[None] for m in jnp.split(mods_all[l, bp:n_c], 6, axis=-1)]
              for l in range(depth)]

    x_s =jnp.pad(x_sample, ((0, 0), (0, rs - dec_seq), (0, 0))).reshape(bs * rs, d)
    cos_s, sin_s = _rope_tables(past_len + jnp.arange(rs, dtype=jnp.int32), dk // 2)

    def ret_sample(layer, proj):
        return _retention(proj, cos_s, sin_s, state_ret[layer], gn_g3[layer:layer + 1], log_g, batch=bs,
                          chunk=rs, valid=dec_seq, n_sub=1)

    def attn_sample(j, q, k_new, v_new):
        pad = ((0, 0), (0, (page - rs) * SB_HEADS), (0, 0))
        kn = jnp.pad(k_new.reshape(bs, rs * SB_HEADS, hd), pad)
        vn = jnp.pad(v_new.reshape(bs, rs * SB_HEADS, hd), pad)
        o = _sb_sample(q.astype(F32).reshape(bs, rs, d), kn, vn, cache_k, cache_v, page_table, b_sb[j],
                       n_q=dec_seq)
        return o.reshape(bs * rs, d).astype(BF16)

    y_s, st_s, k_s, v_s = _trunk(x_s, mods_s, 1, gains, wts, cast, ret_sample, attn_sample, bs * rs, F32)

    cos_p, sin_p = _rope_tables(jnp.arange(seq, dtype=jnp.int32), dk // 2)
    s0_p = jnp.zeros((bp, RET_HEADS, dk, dv), F32)

    def ret_prompt(layer, proj):
        return _retention(proj, cos_p, sin_p, s0_p, gn_g3[layer:layer + 1], log_g, batch=bp,
                          chunk=RET_CHUNK, valid=RET_CHUNK, n_sub=min(8, seq // RET_CHUNK))

    def attn_prompt(j, q, k_new, v_new):
        return _sb_prompt(q, k_new, v_new, b_sb[j], batch=bp)

    y_p, st_p, k_p, v_p = _trunk(x_prompt.reshape(bp * seq, d), mods_p, bp, gains, wts, cast, ret_prompt,
                                 attn_prompt, 1024, BF16)

    def unpad(a, *tail):
        return a.reshape(bs, rs, *tail)[:, :dec_seq]

    return (y_p.reshape(bp, seq, d), unpad(y_s, d), st_p, st_s,
            k_p.reshape(bp, seq, SB_HEADS, hd), v_p.reshape(bp, seq, SB_HEADS, hd),
            unpad(k_s, SB_HEADS, hd), unpad(v_s, SB_HEADS, hd))
```

```python
import functools

import jax
import jax.numpy as jnp
import numpy as np
from jax import lax
from jax.experimental import pallas as pl
from jax.experimental.pallas import tpu as pltpu

F32 = jnp.float32
BF16 = jnp.bfloat16

LANES = 128
SUBLANES = 8
VMEM_LIMIT_BYTES = 56 * 1024 * 1024

RET_HEADS = 8
RET_CHUNK = 128
ROPE_BASE = 10000.0
SB_HEADS = 16
PAGE_SIZE = 128
NORM_EPS = 1e-6
GN_EPS = 1e-6
SAMPLE_ROWS = SUBLANES
HEAD_PITCH = SB_HEADS + SUBLANES
NT_DIMS = (((1,), (1,)), ((), ()))


def _params(*sem):
    return pltpu.CompilerParams(dimension_semantics=sem, vmem_limit_bytes=VMEM_LIMIT_BYTES)


def _mod_spec(mod, tile_rows, rows_per_group):
    d = mod.shape[-1]
    if mod.shape[1] == 1:
        return pl.BlockSpec((1, 1, d), lambda *g: ((g[0] * tile_rows) // rows_per_group, 0, 0))
    return pl.BlockSpec((1, tile_rows, d), lambda *g: (0, g[0], 0))


def _rms(v, g):
    return v * lax.rsqrt(jnp.mean(v * v, axis=-1, keepdims=True) + NORM_EPS) * g


def _ada_kernel(c_ref, w_ref, b_ref, o_ref):
    c = c_ref[...]
    a = (c * jax.nn.sigmoid(c)).astype(BF16)
    o_ref[0] = jnp.dot(a, w_ref[0].astype(BF16), preferred_element_type=F32) + b_ref[0]


def _ada(c, w_ada, b_ada, tn=1024):
    depth, d, n = w_ada.shape
    m = c.shape[0]
    return pl.pallas_call(
        _ada_kernel,
        grid=(depth, n // tn),
        in_specs=[pl.BlockSpec((m, d), lambda l, j: (0, 0)),
                  pl.BlockSpec((1, d, tn), lambda l, j: (l, 0, j)),
                  pl.BlockSpec((1, 1, tn), lambda l, j: (l, 0, j))],
        out_specs=pl.BlockSpec((1, m, tn), lambda l, j: (l, 0, j)),
        out_shape=jax.ShapeDtypeStruct((depth, m, n), F32),
        compiler_params=_params("arbitrary", "arbitrary"),
        name="ada_mods",
    )(c, w_ada, b_ada.reshape(depth, 1, n))


def _norm_kernel(x_ref, g_ref, sc_ref, sh_ref, o_ref):
    y = _rms(x_ref[...], g_ref[0]) * (1.0 + sc_ref[0]) + sh_ref[0]
    o_ref[...] = y.astype(o_ref.dtype)


def _norm(x, g3, layer, sc, sh, groups, tr=512):
    m, d = x.shape
    tr = min(tr, m)
    return pl.pallas_call(
        _norm_kernel,
        grid=(m // tr,),
        in_specs=[pl.BlockSpec((tr, d), lambda i: (i, 0)),
                  pl.BlockSpec((1, 1, d), lambda i: (layer, 0, 0)),
                  _mod_spec(sc, tr, m // groups), _mod_spec(sh, tr, m // groups)],
        out_specs=pl.BlockSpec((tr, d), lambda i: (i, 0)),
        out_shape=jax.ShapeDtypeStruct((m, d), BF16),
        compiler_params=_params("arbitrary"),
        name="norm_mod",
    )(x, g3, sc, sh)


def _mm_kernel(*refs, epilogue, nk, nexts, scale, emit_w, dup):
    a_ref, w_ref = refs[:2]
    pos = 2
    next_in = []
    if epilogue == "resid":
        x_ref, gt_ref, g_ref = refs[2:5]
        pos = 5
        for modulated in nexts:
            n_in = 3 if modulated else 1
            next_in.append(refs[pos:pos + n_in])
            pos += n_in
    o_ref = refs[pos]
    next_out = refs[pos + 1:pos + 1 + len(nexts)]
    pos += 1 + len(nexts)
    dup_ref = refs[pos] if dup else None
    pos += int(dup)
    wb_ref = refs[pos] if emit_w else None
    acc_ref = refs[pos + int(emit_w)] if nk > 1 else None

    def prod():
        if not emit_w:
            return jnp.dot(a_ref[...], w_ref[0], preferred_element_type=F32)
        wb = w_ref[0].astype(BF16)
        wb_ref[0] = wb
        return jnp.dot(a_ref[...], wb, preferred_element_type=F32)

    def finish(acc):
        if epilogue == "relu2":
            r = jnp.maximum(acc, 0.0)
            acc = r * r
        elif epilogue == "scale":
            acc = acc * scale
        elif epilogue == "resid":
            acc = x_ref[...] + gt_ref[0] * _rms(acc, g_ref[0])
            for ins, out in zip(next_in, next_out):
                y = _rms(acc, ins[0][0])
                if len(ins) == 3:
                    y = y * (1.0 + ins[1][0]) + ins[2][0]
                out[...] = y.astype(out.dtype)
        o_ref[...] = acc.astype(o_ref.dtype)
        if dup:
            dup_ref[...] = acc.astype(BF16)

    if nk == 1:
        finish(prod())
        return
    k = pl.program_id(2)

    @pl.when(k == 0)
    def _():
        acc_ref[...] = prod()

    @pl.when(jnp.logical_and(k > 0, k < nk - 1))
    def _():
        acc_ref[...] += prod()

    @pl.when(k == nk - 1)
    def _():
        finish(acc_ref[...] + prod())


def _mm(a, w3, layer, *, out_dtype, n_out=None, col0=0, epilogue="none", scale=1.0, resid=None, nexts=(),
        dup_bf16=False, groups=1, tm=1024, tn=2048, tk=2048):
    m, kdim = a.shape
    n = w3.shape[2] if n_out is None else n_out
    emit_w = w3.dtype == F32
    tm, tn, tk = min(tm, m), min(tn, n), min(tk, kdim)
    if epilogue == "resid":
        tm, tn = min(tm, 512), n
    if emit_w:
        assert tm == m
        tk = min(tk, 1024)
    nk = kdim // tk
    cb = col0 // tn
    rpg = m // groups
    in_specs = [pl.BlockSpec((tm, tk), lambda i, j, k: (i, k)),
                pl.BlockSpec((1, tk, tn), lambda i, j, k: (layer, k, j + cb))]
    args = [a, w3]
    out_specs = [pl.BlockSpec((tm, tn), lambda i, j, k: (i, j))]
    out_shape = [jax.ShapeDtypeStruct((m, n), out_dtype)]
    if epilogue == "resid":
        x, gt, g3, glayer = resid
        in_specs += [pl.BlockSpec((tm, n), lambda i, j, k: (i, 0)), _mod_spec(gt, tm, rpg),
                     pl.BlockSpec((1, 1, n), lambda i, j, k: (glayer, 0, 0))]
        args += [x, gt, g3]
        for ng3, nlayer, sc, sh in nexts:
            in_specs.append(pl.BlockSpec((1, 1, n), lambda i, j, k, nlayer=nlayer: (nlayer, 0, 0)))
            args.append(ng3)
            if sc is not None:
                in_specs += [_mod_spec(sc, tm, rpg), _mod_spec(sh, tm, rpg)]
                args += [sc, sh]
            out_specs.append(pl.BlockSpec((tm, n), lambda i, j, k: (i, 0)))
            out_shape.append(jax.ShapeDtypeStruct((m, n), BF16))
    if dup_bf16:
        out_specs.append(pl.BlockSpec((tm, tn), lambda i, j, k: (i, j)))
        out_shape.append(jax.ShapeDtypeStruct((m, n), BF16))
    if emit_w:
        out_specs.append(pl.BlockSpec((1, tk, tn), lambda i, j, k: (0, k, j)))
        out_shape.append(jax.ShapeDtypeStruct((1, kdim, n), BF16))
    return pl.pallas_call(
        functools.partial(_mm_kernel, epilogue=epilogue, nk=nk, scale=scale, emit_w=emit_w, dup=dup_bf16,
                          nexts=tuple(sc is not None for _, _, sc, _ in nexts)),
        grid=(m // tm, n // tn, nk),
        in_specs=in_specs,
        out_specs=out_specs,
        out_shape=out_shape,
        scratch_shapes=[pltpu.VMEM((tm, tn), F32)] if nk > 1 else [],
        compiler_params=_params("arbitrary", "arbitrary", "arbitrary"),
        name="mm_" + epilogue,
    )(*args)


def _ret_kernel(lg_ref, q_ref, k_ref, v_ref, g_ref, cos_ref, sin_ref, gng_ref, s0_ref,
                y_ref, sfin_ref, s_scr, *, chunk, n_sub, valid, dk, mxu_dtype):
    h = pl.program_id(1)
    c = pl.program_id(2)

    @pl.when(c == 0)
    def _():
        s_scr[...] = s0_ref[0, 0]

    lg = lg_ref[h]
    row = lax.broadcasted_iota(jnp.int32, (chunk, 1), 0).astype(F32)
    col = lax.broadcasted_iota(jnp.int32, (1, chunk), 1).astype(F32)
    diff = row - col
    causal = diff >= 0.0
    decay = jnp.where(causal, jnp.exp(jnp.where(causal, diff, 0.0) * lg), 0.0)
    q_scale = jnp.exp((row + 1.0) * lg)
    k_scale = jnp.where(row < valid, jnp.exp((valid - 1.0 - row) * lg), 0.0)
    s_scale = jnp.exp(jnp.full((1, s_scr.shape[1]), valid, F32) * lg)
    half = dk // 2

    def rot(x, cos, sin):
        x1, x2 = x[:, :half], x[:, half:]
        return jnp.concatenate([x1 * cos - x2 * sin, x1 * sin + x2 * cos], axis=1)

    for sub in range(n_sub):
        rows = pl.ds(sub * chunk, chunk)
        cos, sin = cos_ref[rows, :], sin_ref[rows, :]
        q = rot(q_ref[0, rows, :].astype(F32), cos, sin)
        k = rot(k_ref[0, rows, :].astype(F32), cos, sin) * (dk ** -0.5)
        v = v_ref[0, rows, :].astype(mxu_dtype)
        s = s_scr[...]
        scores = lax.dot_general(q.astype(mxu_dtype), k.astype(mxu_dtype), NT_DIMS,
                                 preferred_element_type=F32) * decay
        o = jnp.dot(scores.astype(mxu_dtype), v, preferred_element_type=F32)
        o = o + jnp.dot((q * q_scale).astype(mxu_dtype), s.astype(mxu_dtype), preferred_element_type=F32)
        kd = (k * k_scale).astype(mxu_dtype)
        s_scr[...] = s_scale * s + lax.dot_general(kd, v, (((0,), (0,)), ((), ())),
                                                   preferred_element_type=F32)
        mu = jnp.mean(o, axis=-1, keepdims=True)
        ctr = o - mu
        var = jnp.mean(ctr * ctr, axis=-1, keepdims=True)
        on = ctr * lax.rsqrt(var + GN_EPS) * gng_ref[0]
        g = g_ref[0, rows, :].astype(F32)
        y_ref[0, rows, :] = (g * jax.nn.sigmoid(g) * on).astype(y_ref.dtype)

    @pl.when(c == pl.num_programs(2) - 1)
    def _():
        sfin_ref[0, 0] = s_scr[...]


def _retention(proj, cos, sin, s0, gn_g3, log_g, *, batch, chunk, valid, n_sub):
    m, six_d = proj.shape
    d = six_d // 6
    t = m // batch
    dk, dv = d // RET_HEADS, 2 * d // RET_HEADS
    rows = chunk * n_sub
    proj3 = proj.reshape(batch, t, six_d)
    mxu_dtype = BF16 if chunk % 16 == 0 else F32
    kern = functools.partial(_ret_kernel, chunk=chunk, n_sub=n_sub, valid=float(valid), dk=dk,
                             mxu_dtype=mxu_dtype)
    y, s_fin = pl.pallas_call(
        kern,
        grid=(batch, RET_HEADS, t // rows),
        in_specs=[pl.BlockSpec(memory_space=pltpu.SMEM),
                  pl.BlockSpec((1, rows, dk), lambda b, h, c: (b, c, h)),
                  pl.BlockSpec((1, rows, dk), lambda b, h, c: (b, c, RET_HEADS + h)),
                  pl.BlockSpec((1, rows, dv), lambda b, h, c: (b, c, RET_HEADS + h)),
                  pl.BlockSpec((1, rows, dv), lambda b, h, c: (b, c, 2 * RET_HEADS + h)),
                  pl.BlockSpec((rows, dk // 2), lambda b, h, c: (c, 0)),
                  pl.BlockSpec((rows, dk // 2), lambda b, h, c: (c, 0)),
                  pl.BlockSpec((1, 1, dv), lambda b, h, c: (0, 0, h)),
                  pl.BlockSpec((1, 1, dk, dv), lambda b, h, c: (b, h, 0, 0))],
        out_specs=[pl.BlockSpec((1, rows, dv), lambda b, h, c: (b, c, h)),
                   pl.BlockSpec((1, 1, dk, dv), lambda b, h, c: (b, h, 0, 0))],
        out_shape=[jax.ShapeDtypeStruct((batch, t, 2 * d), mxu_dtype),
                   jax.ShapeDtypeStruct((batch, RET_HEADS, dk, dv), F32)],
        scratch_shapes=[pltpu.VMEM((dk, dv), F32)],
        compiler_params=_params("arbitrary", "arbitrary", "arbitrary"),
        name="retention",
    )(log_g, proj3, proj3, proj3, proj3, cos, sin, gn_g3, s0)
    return y.reshape(m, 2 * d).astype(BF16), s_fin


def _suffix_matrix():
    j = np.arange(LANES)[:, None]
    c = np.arange(2 * LANES)[None, :]
    return jnp.asarray(np.where(c < LANES, j > c, True), dtype=BF16)


def _sb_logs(z):
    m = jnp.minimum(z, 0.0)
    mz = m - z
    t = jnp.log(1.0 + jnp.exp(m + mz))
    return m - t, mz - t


def _mask_top(x, mask):
    top = jnp.where(mask, x[:LANES], 0.0)
    return top if x.shape[0] == LANES else jnp.concatenate([top, x[LANES:]], axis=0)


def _sb_suffix(log_keep, u):
    r = jnp.dot(log_keep.astype(BF16), u, preferred_element_type=F32)
    return r[:, :LANES], r[:, LANES:]


def _sb_prompt_kernel(bias_ref, u_ref, q_ref, k_ref, v_ref, o_ref, o_scr, c_scr, *, tile, hps):
    hg = pl.program_id(1)
    qi = pl.program_id(2)
    u = u_ref[...]
    nb = tile // LANES
    lane = lax.broadcasted_iota(jnp.int32, (1, LANES), 1)
    q_ext = jnp.broadcast_to(jnp.where(lane < 3, 1.0, 0.0), (tile, LANES)).astype(BF16)
    tri = (lax.broadcasted_iota(jnp.int32, (LANES, LANES), 1)
           < lax.broadcasted_iota(jnp.int32, (LANES, LANES), 0))

    def bias_columns(hh):
        b0 = jnp.full((1, LANES), bias_ref[hg * hps + hh], F32)
        b1 = b0.astype(BF16).astype(F32)
        b2 = (b0 - b1).astype(BF16).astype(F32)
        b3 = (b0 - b1) - b2
        row = jnp.where(lane == 0, b1, jnp.where(lane == 1, b2, jnp.where(lane == 2, b3, 0.0)))
        return jnp.broadcast_to(row, (tile, LANES)).astype(BF16)

    k_ext = [bias_columns(hh) for hh in range(hps)]

    def logits(hh, k0):
        cols = slice(hh * LANES, (hh + 1) * LANES)
        q_aug = jnp.concatenate([q_ref[0, :, cols], q_ext], axis=1)
        k_aug = jnp.concatenate([k_ref[0, pl.ds(k0, tile), cols].astype(BF16), k_ext[hh]], axis=1)
        return lax.dot_general(q_aug, k_aug, NT_DIMS, preferred_element_type=F32)

    def values(hh, k0):
        return v_ref[0, pl.ds(k0, tile), hh * LANES:(hh + 1) * LANES].astype(BF16)

    kd = pl.multiple_of(qi * tile, tile)
    for hh in range(hps):
        z = logits(hh, kd)
        carry = jnp.zeros((tile, LANES), F32)
        wd = []
        for kb in reversed(range(nb)):
            r0 = kb * LANES
            log_beta, log_keep = _sb_logs(z[r0:, r0:r0 + LANES])
            suffix, total = _sb_suffix(_mask_top(log_keep, tri), u)
            c = carry[r0:]
            w = _mask_top(jnp.exp(log_beta + suffix + c), tri).astype(BF16)
            wd.insert(0, w if kb == 0 else jnp.concatenate([jnp.zeros((r0, LANES), BF16), w], axis=0))
            carry = c + total if kb == 0 else jnp.concatenate([carry[:r0], c + total], axis=0)
        c_scr[hh] = carry
        o_scr[hh] = jnp.dot(jnp.concatenate(wd, axis=1), values(hh, kd), preferred_element_type=F32)

    def left_tile(i, _):
        k0 = pl.multiple_of((qi - 1 - i) * tile, tile)
        for hh in range(hps):
            z = logits(hh, k0)
            carry = c_scr[hh]
            ws = [None] * nb
            for kb in reversed(range(nb)):
                log_beta, log_keep = _sb_logs(z[:, kb * LANES:(kb + 1) * LANES])
                suffix, total = _sb_suffix(log_keep, u)
                ws[kb] = jnp.exp(log_beta + suffix + carry).astype(BF16)
                carry = carry + total
            c_scr[hh] = carry
            o_scr[hh] += jnp.dot(jnp.concatenate(ws, axis=1), values(hh, k0), preferred_element_type=F32)
        return 0

    lax.fori_loop(0, qi, left_tile, 0)
    for hh in range(hps):
        o_ref[0, :, hh * LANES:(hh + 1) * LANES] = o_scr[hh].astype(o_ref.dtype)


def _sb_prompt(q, k, v, bias, *, batch, tile=512, hps=8):
    m, d = q.shape
    t = m // batch
    w = hps * (d // SB_HEADS)
    tile = min(tile, t)
    q3, k3, v3 = (a.reshape(batch, t, d) for a in (q, k, v))
    o = pl.pallas_call(
        functools.partial(_sb_prompt_kernel, tile=tile, hps=hps),
        grid=(batch, SB_HEADS // hps, t // tile),
        in_specs=[pl.BlockSpec(memory_space=pltpu.SMEM),
                  pl.BlockSpec((LANES, 2 * LANES), lambda b, h, i: (0, 0)),
                  pl.BlockSpec((1, tile, w), lambda b, h, i: (b, i, h)),
                  pl.BlockSpec((1, t, w), lambda b, h, i: (b, 0, h)),
                  pl.BlockSpec((1, t, w), lambda b, h, i: (b, 0, h))],
        out_specs=pl.BlockSpec((1, tile, w), lambda b, h, i: (b, i, h)),
        out_shape=jax.ShapeDtypeStruct((batch, t, d), BF16),
        scratch_shapes=[pltpu.VMEM((hps, tile, LANES), F32), pltpu.VMEM((hps, tile, LANES), F32)],
        compiler_params=_params("arbitrary", "arbitrary", "arbitrary"),
        name="sb_prompt",
    )(bias, _suffix_matrix(), q3, k3, v3)
    return o.reshape(m, d)


def _sb_sample_kernel(pt_ref, bias_ref, u_ref, q_ref, kn_ref, vn_ref, kc_hbm, vc_hbm, o_ref,
                      c_scr, kbuf, vbuf, sem, *, n_q, pps, n_pages):
    b = pl.program_id(0)
    s = pl.program_id(1)
    n_b = pl.num_programs(0)
    last = pl.num_programs(1) - 1
    rq = SAMPLE_ROWS
    hd = LANES

    def page_copies(bb, ss):
        slot = (ss - 1) % 2
        out = []
        for p in range(pps):
            pg = pt_ref[bb, n_pages - 1 - (ss - 1) * pps - p]
            for i, (hbm, buf) in enumerate(((kc_hbm, kbuf), (vc_hbm, vbuf))):
                out.append(pltpu.make_async_copy(hbm.at[pg], buf.at[slot, p, :, pl.ds(0, SB_HEADS), :],
                                                 sem.at[slot, i, p]))
        return out

    @pl.when(jnp.logical_and(b == 0, s == 0))
    def _():
        for c in page_copies(0, 1):
            c.start()

    @pl.when(jnp.logical_and(s >= 1, s < last))
    def _():
        for c in page_copies(b, s + 1):
            c.start()

    @pl.when(jnp.logical_and(s == last, b < n_b - 1))
    def _():
        for c in page_copies(b + 1, 1):
            c.start()

    @pl.when(s == 0)
    def _():
        o_ref[...] = jnp.zeros_like(o_ref)
        c_scr[...] = jnp.zeros_like(c_scr)

    def page(k_rows, v_rows, masked):
        zs = []
        for h in range(SB_HEADS):
            qh = q_ref[0, :, h * hd:(h + 1) * hd].astype(BF16)
            zh = lax.dot_general(qh, k_rows(h), NT_DIMS, preferred_element_type=F32)
            zs.append(zh + bias_ref[h])
        z = jnp.concatenate(zs, axis=0)
        log_beta, log_keep = _sb_logs(z)
        if masked:
            t_q = lax.broadcasted_iota(jnp.int32, z.shape, 0) % rq
            mask = lax.broadcasted_iota(jnp.int32, z.shape, 1) < jnp.minimum(t_q, n_q)
            log_keep = jnp.where(mask, log_keep, 0.0)
        suffix, total = _sb_suffix(log_keep, u_ref[...])
        carry = c_scr[...]
        w = jnp.exp(log_beta + suffix + carry)
        if masked:
            w = jnp.where(mask, w, 0.0)
        c_scr[...] = carry + total
        for h in range(SB_HEADS):
            o_ref[0, :, h * hd:(h + 1) * hd] += jnp.dot(w[h * rq:(h + 1) * rq, :].astype(BF16), v_rows(h),
                                                       preferred_element_type=F32)

    def strided_rows(ref2d, pitch):
        return lambda h: ref2d[pl.ds(h, PAGE_SIZE, stride=pitch), :].astype(BF16)

    @pl.when(s == 0)
    def _():
        page(strided_rows(kn_ref, SB_HEADS), strided_rows(vn_ref, SB_HEADS), True)

    @pl.when(s > 0)
    def _():
        for c in page_copies(b, s):
            c.wait()
        slot = (s - 1) % 2
        for p in range(pps):
            k2d = kbuf.at[slot, p].reshape(PAGE_SIZE * HEAD_PITCH, hd)
            v2d = vbuf.at[slot, p].reshape(PAGE_SIZE * HEAD_PITCH, hd)
            page(strided_rows(k2d, HEAD_PITCH), strided_rows(v2d, HEAD_PITCH), False)


def _sb_sample(q, k_new_page, v_new_page, cache_k, cache_v, page_table, bias, *, n_q):
    b, n_pages = page_table.shape
    d = q.shape[-1]
    rows, hd = k_new_page.shape[1:]
    pps = next(p for p in (4, 2, 1) if n_pages % p == 0)
    assert (n_pages // pps) % 2 == 0

    new_spec = pl.BlockSpec((None, rows, hd), lambda i, s, pt: (i, 0, 0))
    page_buf = pltpu.VMEM((2, pps, PAGE_SIZE, HEAD_PITCH, hd), F32)
    grid_spec = pltpu.PrefetchScalarGridSpec(
        num_scalar_prefetch=1,
        grid=(b, n_pages // pps + 1),
        in_specs=[pl.BlockSpec(memory_space=pltpu.SMEM),
                  pl.BlockSpec((LANES, 2 * LANES), lambda i, s, pt: (0, 0)),
                  pl.BlockSpec((1, SAMPLE_ROWS, d), lambda i, s, pt: (i, 0, 0)),
                  new_spec, new_spec,
                  pl.BlockSpec(memory_space=pl.ANY), pl.BlockSpec(memory_space=pl.ANY)],
        out_specs=pl.BlockSpec((1, SAMPLE_ROWS, d), lambda i, s, pt: (i, 0, 0)),
        scratch_shapes=[pltpu.VMEM((SB_HEADS * SAMPLE_ROWS, LANES), F32), page_buf, page_buf,
                        pltpu.SemaphoreType.DMA((2, 2, pps))],
    )
    return pl.pallas_call(
        functools.partial(_sb_sample_kernel, n_q=n_q, pps=pps, n_pages=n_pages),
        grid_spec=grid_spec,
        out_shape=jax.ShapeDtypeStruct((b, SAMPLE_ROWS, d), F32),
        compiler_params=_params("arbitrary", "arbitrary"),
        name="sb_sample",
    )(page_table, bias, _suffix_matrix(), q, k_new_page, v_new_page, cache_k, cache_v)


def _rope_tables(pos, half):
    freqs = ROPE_BASE ** (-jnp.arange(half, dtype=F32) / half)
    ang = pos.astype(F32)[:, None] * freqs[None, :]
    return jnp.cos(ang), jnp.sin(ang)


def _trunk(x, mods, groups, gains, wts, cast, retention_fn, attention_fn, tm, proj_dtype, kv_bf16):
    g_pre_mix, g_post_mix, g_pre_ffn, g_post_ffn, kv_norm_g = gains
    d = x.shape[1]
    n_a = wts["a_w_in"].shape[0]
    depth = wts["w_ff1"].shape[0]
    hd = d // SB_HEADS

    def mm(a, name, layer, col0=0, **kw):
        key = (name, layer, col0)
        if key in cast:
            return _mm(a, cast[key], 0, tm=tm, **kw)
        *outs, cast[key] = _mm(a, wts[name], layer, col0=col0, tm=tm, **kw)
        return outs

    states = []
    k_new = v_new = None
    h = _norm(x, g_pre_mix, 0, mods[0][1], mods[0][0], groups)
    hn = None
    for layer in range(depth):
        sh1, sc1, gt1, sh2, sc2, gt2 = mods[layer]
        if layer < n_a:
            proj, = mm(h, "a_w_in", layer, out_dtype=proj_dtype)
            y, s_fin = retention_fn(layer, proj)
            states.append(s_fin)
            w_out, j = "a_w_out", layer
        else:
            if k_new is None:
                k_new, *k_att = mm(hn, "w_kv", 0, col0=0, out_dtype=F32, n_out=d, dup_bf16=kv_bf16)
                v_new, *v_att = mm(hn, "w_kv", 0, col0=d, out_dtype=F32, n_out=d, dup_bf16=kv_bf16)
            j = layer - n_a
            q, = mm(h, "b_w_q", j, out_dtype=BF16, epilogue="scale", scale=hd ** -0.5)
            y = attention_fn(j, q, *(k_att or [k_new]), *(v_att or [v_new]))
            w_out = "b_w_out"
        x, h = mm(y, w_out, j, out_dtype=F32, epilogue="resid", resid=(x, gt1, g_post_mix, layer),
                  nexts=((g_pre_ffn, layer, sc2, sh2),), groups=groups)
        f, = mm(h, "w_ff1", layer, out_dtype=BF16, epilogue="relu2")
        nexts = []
        if layer + 1 < depth:
            nexts.append((g_pre_mix, layer + 1, mods[layer + 1][1], mods[layer + 1][0]))
        if layer + 1 == n_a:
            nexts.append((kv_norm_g, 0, None, None))
        outs = mm(f, "w_ff2", layer, out_dtype=F32, epilogue="resid", resid=(x, gt2, g_post_ffn, layer),
                  nexts=tuple(nexts), groups=groups)
        x = outs[0]
        if layer + 1 < depth:
            h = outs[1]
        if layer + 1 == n_a:
            hn = outs[-1]
    return x, jnp.stack(states, axis=0), k_new, v_new


def kernel(x_prompt, x_sample, c_prompt, c_sample, state_ret, cache_k, cache_v, page_table, w_ada, b_ada,
           g_pre_mix, g_post_mix, g_pre_ffn, g_post_ffn, w_ff1, w_ff2, a_w_in, a_w_out, a_gn_g, kv_norm_g,
           w_kv, b_w_q, b_w_out, b_sb):
    bp, seq, d = x_prompt.shape
    bs, dec_seq, _ = x_sample.shape
    depth = w_ada.shape[0]
    n_pool, page, _, _ = cache_k.shape
    n_pages = page_table.shape[1]
    past_len = n_pages * page
    hd = d // SB_HEADS
    dk, dv = d // RET_HEADS, 2 * d // RET_HEADS
    rs = SAMPLE_ROWS

    gains = (g_pre_mix.reshape(depth, 1, d), g_post_mix.reshape(depth, 1, d),
             g_pre_ffn.reshape(depth, 1, d), g_post_ffn.reshape(depth, 1, d), kv_norm_g.reshape(1, 1, d))
    wts = dict(w_ff1=w_ff1, w_ff2=w_ff2, a_w_in=a_w_in, a_w_out=a_w_out, w_kv=w_kv[None], b_w_q=b_w_q,
               b_w_out=b_w_out)
    cast = {}
    gn_g3 = a_gn_g.reshape(a_gn_g.shape[0], 1, 2 * d)
    log_g = jnp.log1p(-jnp.exp2(-5.0 - jnp.arange(RET_HEADS, dtype=F32)))

    n_c = bp + bs
    c_rows = -(-n_c // SUBLANES) * SUBLANES
    c_all = jnp.concatenate([c_prompt, c_sample, jnp.zeros((c_rows - n_c, d), F32)], axis=0)
    mods_all = _ada(c_all, w_ada, b_ada)
    mods_p = [[m[:, None, :] for m in jnp.split(mods_all[l, :bp], 6, axis=-1)] for l in range(depth)]
    mods_s = [[jnp.repeat(m, rs, axis=0)[None] for m in jnp.split(mods_all[l, bp:n_c], 6, axis=-1)]
              for l in range(depth)]

    x_s =jnp.pad(x_sample, ((0, 0), (0, rs - dec_seq), (0, 0))).reshape(bs * rs, d)
    cos_s, sin_s = _rope_tables(past_len + jnp.arange(rs, dtype=jnp.int32), dk // 2)

    def ret_sample(layer, proj):
        return _retention(proj, cos_s, sin_s, state_ret[layer], gn_g3[layer:layer + 1], log_g, batch=bs,
                          chunk=rs, valid=dec_seq, n_sub=1)

    def attn_sample(j, q, k_new, v_new):
        pad = ((0, 0), (0, (page - rs) * SB_HEADS), (0, 0))
        kn = jnp.pad(k_new.reshape(bs, rs * SB_HEADS, hd), pad)
        vn = jnp.pad(v_new.reshape(bs, rs * SB_HEADS, hd), pad)
        o = _sb_sample(q.astype(F32).reshape(bs, rs, d), kn, vn, cache_k, cache_v, page_table, b_sb[j],
                       n_q=dec_seq)
        return o.reshape(bs * rs, d).astype(BF16)

    y_s, st_s, k_s, v_s = _trunk(x_s, mods_s, 1, gains, wts, cast, ret_sample, attn_sample, bs * rs, F32,
                                 False)

    cos_p, sin_p = _rope_tables(jnp.arange(seq, dtype=jnp.int32), dk // 2)
    s0_p = jnp.zeros((bp, RET_HEADS, dk, dv), F32)

    def ret_prompt(layer, proj):
        return _retention(proj, cos_p, sin_p, s0_p, gn_g3[layer:layer + 1], log_g, batch=bp,
                          chunk=RET_CHUNK, valid=RET_CHUNK, n_sub=min(8, seq // RET_CHUNK))

    def attn_prompt(j, q, k_new, v_new):
        return _sb_prompt(q, k_new, v_new, b_sb[j], batch=bp)

    y_p, st_p, k_p, v_p = _trunk(x_prompt.reshape(bp * seq, d), mods_p, bp, gains, wts, cast, ret_prompt,
                                 attn_prompt, 1024, BF16, True)

    def unpad(a, *tail):
        return a.reshape(bs, rs, *tail)[:, :dec_seq]

    return (y_p.reshape(bp, seq, d), unpad(y_s, d), st_p, st_s,
            k_p.reshape(bp, seq, SB_HEADS, hd), v_p.reshape(bp, seq, SB_HEADS, hd),
            unpad(k_s, SB_HEADS, hd), unpad(v_s, SB_HEADS, hd))
```

```python
import functools

import jax
import jax.numpy as jnp
import numpy as np
from jax import lax
from jax.experimental import pallas as pl
from jax.experimental.pallas import tpu as pltpu

F32 = jnp.float32
BF16 = jnp.bfloat16

LANES = 128
SUBLANES = 8
VMEM_LIMIT_BYTES = 56 * 1024 * 1024

RET_HEADS = 8
RET_CHUNK = 128
ROPE_BASE = 10000.0
SB_HEADS = 16
PAGE_SIZE = 128
NORM_EPS = 1e-6
GN_EPS = 1e-6
SAMPLE_ROWS = SUBLANES
HEAD_PITCH = SB_HEADS + SUBLANES
PAGE_SLOTS = 3
NT_DIMS = (((1,), (1,)), ((), ()))


def _params(*sem):
    return pltpu.CompilerParams(dimension_semantics=sem, vmem_limit_bytes=VMEM_LIMIT_BYTES)


def _mod_spec(mod, tile_rows, rows_per_group):
    d = mod.shape[-1]
    if mod.shape[1] == 1:
        return pl.BlockSpec((1, 1, d), lambda *g: ((g[0] * tile_rows) // rows_per_group, 0, 0))
    return pl.BlockSpec((1, tile_rows, d), lambda *g: (0, g[0], 0))


def _rms(v, g):
    return v * lax.rsqrt(jnp.mean(v * v, axis=-1, keepdims=True) + NORM_EPS) * g


def _ada_kernel(c_ref, w_ref, b_ref, o_ref):
    c = c_ref[...]
    a = (c * jax.nn.sigmoid(c)).astype(BF16)
    o_ref[0] = jnp.dot(a, w_ref[0].astype(BF16), preferred_element_type=F32) + b_ref[0]


def _ada(c, w_ada, b_ada, tn=1024):
    depth, d, n = w_ada.shape
    m = c.shape[0]
    return pl.pallas_call(
        _ada_kernel,
        grid=(depth, n // tn),
        in_specs=[pl.BlockSpec((m, d), lambda l, j: (0, 0)),
                  pl.BlockSpec((1, d, tn), lambda l, j: (l, 0, j)),
                  pl.BlockSpec((1, 1, tn), lambda l, j: (l, 0, j))],
        out_specs=pl.BlockSpec((1, m, tn), lambda l, j: (l, 0, j)),
        out_shape=jax.ShapeDtypeStruct((depth, m, n), F32),
        compiler_params=_params("arbitrary", "arbitrary"),
        name="ada_mods",
    )(c, w_ada, b_ada.reshape(depth, 1, n))


def _norm_kernel(x_ref, g_ref, sc_ref, sh_ref, o_ref):
    y = _rms(x_ref[...], g_ref[0]) * (1.0 + sc_ref[0]) + sh_ref[0]
    o_ref[...] = y.astype(o_ref.dtype)


def _norm(x, g3, layer, sc, sh, groups, tr=512):
    m, d = x.shape
    tr = min(tr, m)
    return pl.pallas_call(
        _norm_kernel,
        grid=(m // tr,),
        in_specs=[pl.BlockSpec((tr, d), lambda i: (i, 0)),
                  pl.BlockSpec((1, 1, d), lambda i: (layer, 0, 0)),
                  _mod_spec(sc, tr, m // groups), _mod_spec(sh, tr, m // groups)],
        out_specs=pl.BlockSpec((tr, d), lambda i: (i, 0)),
        out_shape=jax.ShapeDtypeStruct((m, d), BF16),
        compiler_params=_params("arbitrary"),
        name="norm_mod",
    )(x, g3, sc, sh)


def _mm_kernel(*refs, epilogue, nk, nexts, scale, emit_w, dup):
    a_ref, w_ref = refs[:2]
    pos = 2
    next_in = []
    if epilogue == "resid":
        x_ref, gt_ref, g_ref = refs[2:5]
        pos = 5
        for modulated in nexts:
            n_in = 3 if modulated else 1
            next_in.append(refs[pos:pos + n_in])
            pos += n_in
    o_ref = refs[pos]
    next_out = refs[pos + 1:pos + 1 + len(nexts)]
    pos += 1 + len(nexts)
    dup_ref = refs[pos] if dup else None
    pos += int(dup)
    wb_ref = refs[pos] if emit_w else None
    acc_ref = refs[pos + int(emit_w)] if nk > 1 else None

    def prod():
        if not emit_w:
            return jnp.dot(a_ref[...], w_ref[0], preferred_element_type=F32)
        wb = w_ref[0].astype(BF16)
        wb_ref[0] = wb
        return jnp.dot(a_ref[...], wb, preferred_element_type=F32)

    def finish(acc):
        if epilogue == "relu2":
            r = jnp.maximum(acc, 0.0)
            acc = r * r
        elif epilogue == "scale":
            acc = acc * scale
        elif epilogue == "resid":
            acc = x_ref[...] + gt_ref[0] * _rms(acc, g_ref[0])
            for ins, out in zip(next_in, next_out):
                y = _rms(acc, ins[0][0])
                if len(ins) == 3:
                    y = y * (1.0 + ins[1][0]) + ins[2][0]
                out[...] = y.astype(out.dtype)
        o_ref[...] = acc.astype(o_ref.dtype)
        if dup:
            dup_ref[...] = acc.astype(BF16)

    if nk == 1:
        finish(prod())
        return
    k = pl.program_id(2)

    @pl.when(k == 0)
    def _():
        acc_ref[...] = prod()

    @pl.when(jnp.logical_and(k > 0, k < nk - 1))
    def _():
        acc_ref[...] += prod()

    @pl.when(k == nk - 1)
    def _():
        finish(acc_ref[...] + prod())


def _mm(a, w3, layer, *, out_dtype, n_out=None, col0=0, epilogue="none", scale=1.0, resid=None, nexts=(),
        dup_bf16=False, groups=1, tm=1024, tn=2048, tk=2048):
    m, kdim = a.shape
    n = w3.shape[2] if n_out is None else n_out
    emit_w = w3.dtype == F32
    tm, tn, tk = min(tm, m), min(tn, n), min(tk, kdim)
    if epilogue == "resid":
        tm, tn = min(tm, 512), n
    if emit_w:
        assert tm == m
        tk = min(tk, 1024)
    nk = kdim // tk
    cb = col0 // tn
    rpg = m // groups
    in_specs = [pl.BlockSpec((tm, tk), lambda i, j, k: (i, k)),
                pl.BlockSpec((1, tk, tn), lambda i, j, k: (layer, k, j + cb))]
    args = [a, w3]
    out_specs = [pl.BlockSpec((tm, tn), lambda i, j, k: (i, j))]
    out_shape = [jax.ShapeDtypeStruct((m, n), out_dtype)]
    if epilogue == "resid":
        x, gt, g3, glayer = resid
        in_specs += [pl.BlockSpec((tm, n), lambda i, j, k: (i, 0)), _mod_spec(gt, tm, rpg),
                     pl.BlockSpec((1, 1, n), lambda i, j, k: (glayer, 0, 0))]
        args += [x, gt, g3]
        for ng3, nlayer, sc, sh in nexts:
            in_specs.append(pl.BlockSpec((1, 1, n), lambda i, j, k, nlayer=nlayer: (nlayer, 0, 0)))
            args.append(ng3)
            if sc is not None:
                in_specs += [_mod_spec(sc, tm, rpg), _mod_spec(sh, tm, rpg)]
                args += [sc, sh]
            out_specs.append(pl.BlockSpec((tm, n), lambda i, j, k: (i, 0)))
            out_shape.append(jax.ShapeDtypeStruct((m, n), BF16))
    if dup_bf16:
        out_specs.append(pl.BlockSpec((tm, tn), lambda i, j, k: (i, j)))
        out_shape.append(jax.ShapeDtypeStruct((m, n), BF16))
    if emit_w:
        out_specs.append(pl.BlockSpec((1, tk, tn), lambda i, j, k: (0, k, j)))
        out_shape.append(jax.ShapeDtypeStruct((1, kdim, n), BF16))
    return pl.pallas_call(
        functools.partial(_mm_kernel, epilogue=epilogue, nk=nk, scale=scale, emit_w=emit_w, dup=dup_bf16,
                          nexts=tuple(sc is not None for _, _, sc, _ in nexts)),
        grid=(m // tm, n // tn, nk),
        in_specs=in_specs,
        out_specs=out_specs,
        out_shape=out_shape,
        scratch_shapes=[pltpu.VMEM((tm, tn), F32)] if nk > 1 else [],
        compiler_params=_params("arbitrary", "arbitrary", "arbitrary"),
        name="mm_" + epilogue,
    )(*args)


def _ret_kernel(lg_ref, q_ref, k_ref, v_ref, g_ref, cos_ref, sin_ref, gng_ref, s0_ref,
                y_ref, sfin_ref, s_scr, *, chunk, n_sub, valid, dk, mxu_dtype):
    h = pl.program_id(1)
    c = pl.program_id(2)

    @pl.when(c == 0)
    def _():
        s_scr[...] = s0_ref[0, 0]

    lg = lg_ref[h]
    row = lax.broadcasted_iota(jnp.int32, (chunk, 1), 0).astype(F32)
    col = lax.broadcasted_iota(jnp.int32, (1, chunk), 1).astype(F32)
    diff = row - col
    causal = diff >= 0.0
    decay = jnp.where(causal, jnp.exp(jnp.where(causal, diff, 0.0) * lg), 0.0)
    q_scale = jnp.exp((row + 1.0) * lg)
    k_scale = jnp.where(row < valid, jnp.exp((valid - 1.0 - row) * lg), 0.0)
    s_scale = jnp.exp(jnp.full((1, s_scr.shape[1]), valid, F32) * lg)
    half = dk // 2

    def rot(x, cos, sin):
        x1, x2 = x[:, :half], x[:, half:]
        return jnp.concatenate([x1 * cos - x2 * sin, x1 * sin + x2 * cos], axis=1)

    for sub in range(n_sub):
        rows = pl.ds(sub * chunk, chunk)
        cos, sin = cos_ref[rows, :], sin_ref[rows, :]
        q = rot(q_ref[0, rows, :].astype(F32), cos, sin)
        k = rot(k_ref[0, rows, :].astype(F32), cos, sin) * (dk ** -0.5)
        v = v_ref[0, rows, :].astype(mxu_dtype)
        s = s_scr[...]
        scores = lax.dot_general(q.astype(mxu_dtype), k.astype(mxu_dtype), NT_DIMS,
                                 preferred_element_type=F32) * decay
        o = jnp.dot(scores.astype(mxu_dtype), v, preferred_element_type=F32)
        o = o + jnp.dot((q * q_scale).astype(mxu_dtype), s.astype(mxu_dtype), preferred_element_type=F32)
        kd = (k * k_scale).astype(mxu_dtype)
        s_scr[...] = s_scale * s + lax.dot_general(kd, v, (((0,), (0,)), ((), ())),
                                                   preferred_element_type=F32)
        mu = jnp.mean(o, axis=-1, keepdims=True)
        ctr = o - mu
        var = jnp.mean(ctr * ctr, axis=-1, keepdims=True)
        on = ctr * lax.rsqrt(var + GN_EPS) * gng_ref[0]
        g = g_ref[0, rows, :].astype(F32)
        y_ref[0, rows, :] = (g * jax.nn.sigmoid(g) * on).astype(y_ref.dtype)

    @pl.when(c == pl.num_programs(2) - 1)
    def _():
        sfin_ref[0, 0] = s_scr[...]


def _retention(proj, cos, sin, s0, gn_g3, log_g, *, batch, chunk, valid, n_sub):
    m, six_d = proj.shape
    d = six_d // 6
    t = m // batch
    dk, dv = d // RET_HEADS, 2 * d // RET_HEADS
    rows = chunk * n_sub
    proj3 = proj.reshape(batch, t, six_d)
    mxu_dtype = BF16 if chunk % 16 == 0 else F32
    kern = functools.partial(_ret_kernel, chunk=chunk, n_sub=n_sub, valid=float(valid), dk=dk,
                             mxu_dtype=mxu_dtype)
    y, s_fin = pl.pallas_call(
        kern,
        grid=(batch, RET_HEADS, t // rows),
        in_specs=[pl.BlockSpec(memory_space=pltpu.SMEM),
                  pl.BlockSpec((1, rows, dk), lambda b, h, c: (b, c, h)),
                  pl.BlockSpec((1, rows, dk), lambda b, h, c: (b, c, RET_HEADS + h)),
                  pl.BlockSpec((1, rows, dv), lambda b, h, c: (b, c, RET_HEADS + h)),
                  pl.BlockSpec((1, rows, dv), lambda b, h, c: (b, c, 2 * RET_HEADS + h)),
                  pl.BlockSpec((rows, dk // 2), lambda b, h, c: (c, 0)),
                  pl.BlockSpec((rows, dk // 2), lambda b, h, c: (c, 0)),
                  pl.BlockSpec((1, 1, dv), lambda b, h, c: (0, 0, h)),
                  pl.BlockSpec((1, 1, dk, dv), lambda b, h, c: (b, h, 0, 0))],
        out_specs=[pl.BlockSpec((1, rows, dv), lambda b, h, c: (b, c, h)),
                   pl.BlockSpec((1, 1, dk, dv), lambda b, h, c: (b, h, 0, 0))],
        out_shape=[jax.ShapeDtypeStruct((batch, t, 2 * d), mxu_dtype),
                   jax.ShapeDtypeStruct((batch, RET_HEADS, dk, dv), F32)],
        scratch_shapes=[pltpu.VMEM((dk, dv), F32)],
        compiler_params=_params("arbitrary", "arbitrary", "arbitrary"),
        name="retention",
    )(log_g, proj3, proj3, proj3, proj3, cos, sin, gn_g3, s0)
    return y.reshape(m, 2 * d).astype(BF16), s_fin


def _suffix_matrix():
    j = np.arange(LANES)[:, None]
    c = np.arange(2 * LANES)[None, :]
    return jnp.asarray(np.where(c < LANES, j > c, True), dtype=BF16)


def _sb_logs(z):
    m = jnp.minimum(z, 0.0)
    mz = m - z
    t = jnp.log(1.0 + jnp.exp(m + mz))
    return m - t, mz - t


def _mask_top(x, mask):
    top = jnp.where(mask, x[:LANES], 0.0)
    return top if x.shape[0] == LANES else jnp.concatenate([top, x[LANES:]], axis=0)


def _sb_suffix(log_keep, u):
    r = jnp.dot(log_keep.astype(BF16), u, preferred_element_type=F32)
    return r[:, :LANES], r[:, LANES:]


def _sb_prompt_kernel(bias_ref, u_ref, q_ref, k_ref, v_ref, o_ref, o_scr, c_scr, *, tile, hps):
    hg = pl.program_id(1)
    qi = pl.program_id(2)
    u = u_ref[...]
    nb = tile // LANES
    lane = lax.broadcasted_iota(jnp.int32, (1, LANES), 1)
    q_ext = jnp.broadcast_to(jnp.where(lane < 3, 1.0, 0.0), (tile, LANES)).astype(BF16)
    tri = (lax.broadcasted_iota(jnp.int32, (LANES, LANES), 1)
           < lax.broadcasted_iota(jnp.int32, (LANES, LANES), 0))

    def bias_columns(hh):
        b0 = jnp.full((1, LANES), bias_ref[hg * hps + hh], F32)
        b1 = b0.astype(BF16).astype(F32)
        b2 = (b0 - b1).astype(BF16).astype(F32)
        b3 = (b0 - b1) - b2
        row = jnp.where(lane == 0, b1, jnp.where(lane == 1, b2, jnp.where(lane == 2, b3, 0.0)))
        return jnp.broadcast_to(row, (tile, LANES)).astype(BF16)

    k_ext = [bias_columns(hh) for hh in range(hps)]

    def logits(hh, k0):
        cols = slice(hh * LANES, (hh + 1) * LANES)
        q_aug = jnp.concatenate([q_ref[0, :, cols], q_ext], axis=1)
        k_aug = jnp.concatenate([k_ref[0, pl.ds(k0, tile), cols].astype(BF16), k_ext[hh]], axis=1)
        return lax.dot_general(q_aug, k_aug, NT_DIMS, preferred_element_type=F32)

    def values(hh, k0):
        return v_ref[0, pl.ds(k0, tile), hh * LANES:(hh + 1) * LANES].astype(BF16)

    kd = pl.multiple_of(qi * tile, tile)
    for hh in range(hps):
        z = logits(hh, kd)
        carry = jnp.zeros((tile, LANES), F32)
        wd = []
        for kb in reversed(range(nb)):
            r0 = kb * LANES
            log_beta, log_keep = _sb_logs(z[r0:, r0:r0 + LANES])
            suffix, total = _sb_suffix(_mask_top(log_keep, tri), u)
            c = carry[r0:]
            w = _mask_top(jnp.exp(log_beta + suffix + c), tri).astype(BF16)
            wd.insert(0, w if kb == 0 else jnp.concatenate([jnp.zeros((r0, LANES), BF16), w], axis=0))
            carry = c + total if kb == 0 else jnp.concatenate([carry[:r0], c + total], axis=0)
        c_scr[hh] = carry
        o_scr[hh] = jnp.dot(jnp.concatenate(wd, axis=1), values(hh, kd), preferred_element_type=F32)

    def left_tile(i, _):
        k0 = pl.multiple_of((qi - 1 - i) * tile, tile)
        for hh in range(hps):
            z = logits(hh, k0)
            carry = c_scr[hh]
            ws = [None] * nb
            for kb in reversed(range(nb)):
                log_beta, log_keep = _sb_logs(z[:, kb * LANES:(kb + 1) * LANES])
                suffix, total = _sb_suffix(log_keep, u)
                ws[kb] = jnp.exp(log_beta + suffix + carry).astype(BF16)
                carry = carry + total
            c_scr[hh] = carry
            o_scr[hh] += jnp.dot(jnp.concatenate(ws, axis=1), values(hh, k0), preferred_element_type=F32)
        return 0

    lax.fori_loop(0, qi, left_tile, 0)
    for hh in range(hps):
        o_ref[0, :, hh * LANES:(hh + 1) * LANES] = o_scr[hh].astype(o_ref.dtype)


def _sb_prompt(q, k, v, bias, *, batch, tile=512, hps=8):
    m, d = q.shape
    t = m // batch
    w = hps * (d // SB_HEADS)
    tile = min(tile, t)
    q3, k3, v3 = (a.reshape(batch, t, d) for a in (q, k, v))
    o = pl.pallas_call(
        functools.partial(_sb_prompt_kernel, tile=tile, hps=hps),
        grid=(batch, SB_HEADS // hps, t // tile),
        in_specs=[pl.BlockSpec(memory_space=pltpu.SMEM),
                  pl.BlockSpec((LANES, 2 * LANES), lambda b, h, i: (0, 0)),
                  pl.BlockSpec((1, tile, w), lambda b, h, i: (b, i, h)),
                  pl.BlockSpec((1, t, w), lambda b, h, i: (b, 0, h)),
                  pl.BlockSpec((1, t, w), lambda b, h, i: (b, 0, h))],
        out_specs=pl.BlockSpec((1, tile, w), lambda b, h, i: (b, i, h)),
        out_shape=jax.ShapeDtypeStruct((batch, t, d), BF16),
        scratch_shapes=[pltpu.VMEM((hps, tile, LANES), F32), pltpu.VMEM((hps, tile, LANES), F32)],
        compiler_params=_params("arbitrary", "arbitrary", "arbitrary"),
        name="sb_prompt",
    )(bias, _suffix_matrix(), q3, k3, v3)
    return o.reshape(m, d)


def _sb_sample_kernel(pt_ref, bias_ref, u_ref, q_ref, kn_ref, vn_ref, kc_hbm, vc_hbm, o_ref,
                      c_scr, kbuf, vbuf, sem, *, n_q, pps, n_pages, n_seq):
    b = pl.program_id(0)
    s = pl.program_id(1)
    steps = n_pages // pps
    total = n_seq * steps
    lin = b * steps + s - 1
    rq = SAMPLE_ROWS
    hd = LANES

    def page_copies(i):
        bb, ss, slot = i // steps, i % steps, i % PAGE_SLOTS
        out = []
        for p in range(pps):
            pg = pt_ref[bb, n_pages - 1 - ss * pps - p]
            for j, (hbm, buf) in enumerate(((kc_hbm, kbuf), (vc_hbm, vbuf))):
                out.append(pltpu.make_async_copy(hbm.at[pg], buf.at[slot, p, :, pl.ds(0, SB_HEADS), :],
                                                 sem.at[slot, j, p]))
        return out

    @pl.when(jnp.logical_and(b == 0, s == 0))
    def _():
        for i in range(min(PAGE_SLOTS - 1, total)):
            for c in page_copies(i):
                c.start()

    @pl.when(jnp.logical_and(s >= 1, lin + PAGE_SLOTS - 1 < total))
    def _():
        for c in page_copies(lin + PAGE_SLOTS - 1):
            c.start()

    @pl.when(s == 0)
    def _():
        o_ref[...] = jnp.zeros_like(o_ref)
        c_scr[...] = jnp.zeros_like(c_scr)

    def page(k_rows, v_rows, masked):
        zs = []
        for h in range(SB_HEADS):
            qh = q_ref[0, :, h * hd:(h + 1) * hd].astype(BF16)
            zh = lax.dot_general(qh, k_rows(h), NT_DIMS, preferred_element_type=F32)
            zs.append(zh + bias_ref[h])
        z = jnp.concatenate(zs, axis=0)
        log_beta, log_keep = _sb_logs(z)
        if masked:
            t_q = lax.broadcasted_iota(jnp.int32, z.shape, 0) % rq
            mask = lax.broadcasted_iota(jnp.int32, z.shape, 1) < jnp.minimum(t_q, n_q)
            log_keep = jnp.where(mask, log_keep, 0.0)
        suffix, total = _sb_suffix(log_keep, u_ref[...])
        carry = c_scr[...]
        w = jnp.exp(log_beta + suffix + carry)
        if masked:
            w = jnp.where(mask, w, 0.0)
        c_scr[...] = carry + total
        for h in range(SB_HEADS):
            o_ref[0, :, h * hd:(h + 1) * hd] += jnp.dot(w[h * rq:(h + 1) * rq, :].astype(BF16), v_rows(h),
                                                       preferred_element_type=F32)

    def strided_rows(ref2d, pitch):
        return lambda h: ref2d[pl.ds(h, PAGE_SIZE, stride=pitch), :].astype(BF16)

    @pl.when(s == 0)
    def _():
        page(strided_rows(kn_ref, SB_HEADS), strided_rows(vn_ref, SB_HEADS), True)

    @pl.when(s > 0)
    def _():
        for c in page_copies(lin):
            c.wait()
        slot = lin % PAGE_SLOTS
        for p in range(pps):
            k2d = kbuf.at[slot, p].reshape(PAGE_SIZE * HEAD_PITCH, hd)
            v2d = vbuf.at[slot, p].reshape(PAGE_SIZE * HEAD_PITCH, hd)
            page(strided_rows(k2d, HEAD_PITCH), strided_rows(v2d, HEAD_PITCH), False)


def _sb_sample(q, k_new_page, v_new_page, cache_k, cache_v, page_table, bias, *, n_q):
    b, n_pages = page_table.shape
    d = q.shape[-1]
    rows, hd = k_new_page.shape[1:]
    pps = next(p for p in (4, 2, 1) if n_pages % p == 0)

    new_spec = pl.BlockSpec((None, rows, hd), lambda i, s, pt: (i, 0, 0))
    page_buf = pltpu.VMEM((PAGE_SLOTS, pps, PAGE_SIZE, HEAD_PITCH, hd), F32)
    grid_spec = pltpu.PrefetchScalarGridSpec(
        num_scalar_prefetch=1,
        grid=(b, n_pages // pps + 1),
        in_specs=[pl.BlockSpec(memory_space=pltpu.SMEM),
                  pl.BlockSpec((LANES, 2 * LANES), lambda i, s, pt: (0, 0)),
                  pl.BlockSpec((1, SAMPLE_ROWS, d), lambda i, s, pt: (i, 0, 0)),
                  new_spec, new_spec,
                  pl.BlockSpec(memory_space=pl.ANY), pl.BlockSpec(memory_space=pl.ANY)],
        out_specs=pl.BlockSpec((1, SAMPLE_ROWS, d), lambda i, s, pt: (i, 0, 0)),
        scratch_shapes=[pltpu.VMEM((SB_HEADS * SAMPLE_ROWS, LANES), F32), page_buf, page_buf,
                        pltpu.SemaphoreType.DMA((PAGE_SLOTS, 2, pps))],
    )
    return pl.pallas_call(
        functools.partial(_sb_sample_kernel, n_q=n_q, pps=pps, n_pages=n_pages, n_seq=b),
        grid_spec=grid_spec,
        out_shape=jax.ShapeDtypeStruct((b, SAMPLE_ROWS, d), F32),
        compiler_params=_params("arbitrary", "arbitrary"),
        name="sb_sample",
    )(page_table, bias, _suffix_matrix(), q, k_new_page, v_new_page, cache_k, cache_v)


def _rope_tables(pos, half):
    freqs = ROPE_BASE ** (-jnp.arange(half, dtype=F32) / half)
    ang = pos.astype(F32)[:, None] * freqs[None, :]
    return jnp.cos(ang), jnp.sin(ang)


def _trunk(x, mods, groups, gains, wts, cast, retention_fn, attention_fn, tm, proj_dtype, kv_bf16):
    g_pre_mix, g_post_mix, g_pre_ffn, g_post_ffn, kv_norm_g = gains
    d = x.shape[1]
    n_a = wts["a_w_in"].shape[0]
    depth = wts["w_ff1"].shape[0]
    hd = d // SB_HEADS

    def mm(a, name, layer, col0=0, **kw):
        key = (name, layer, col0)
        if key in cast:
            return _mm(a, cast[key], 0, tm=tm, **kw)
        *outs, cast[key] = _mm(a, wts[name], layer, col0=col0, tm=tm, **kw)
        return outs

    states = []
    k_new = v_new = None
    h = _norm(x, g_pre_mix, 0, mods[0][1], mods[0][0], groups)
    hn = None
    for layer in range(depth):
        sh1, sc1, gt1, sh2, sc2, gt2 = mods[layer]
        if layer < n_a:
            proj, = mm(h, "a_w_in", layer, out_dtype=proj_dtype)
            y, s_fin = retention_fn(layer, proj)
            states.append(s_fin)
            w_out, j = "a_w_out", layer
        else:
            if k_new is None:
                k_new, *k_att = mm(hn, "w_kv", 0, col0=0, out_dtype=F32, n_out=d, dup_bf16=kv_bf16)
                v_new, *v_att = mm(hn, "w_kv", 0, col0=d, out_dtype=F32, n_out=d, dup_bf16=kv_bf16)
            j = layer - n_a
            q, = mm(h, "b_w_q", j, out_dtype=BF16, epilogue="scale", scale=hd ** -0.5)
            y = attention_fn(j, q, *(k_att or [k_new]), *(v_att or [v_new]))
            w_out = "b_w_out"
        x, h = mm(y, w_out, j, out_dtype=F32, epilogue="resid", resid=(x, gt1, g_post_mix, layer),
                  nexts=((g_pre_ffn, layer, sc2, sh2),), groups=groups)
        f, = mm(h, "w_ff1", layer, out_dtype=BF16, epilogue="relu2")
        nexts = []
        if layer + 1 < depth:
            nexts.append((g_pre_mix, layer + 1, mods[layer + 1][1], mods[layer + 1][0]))
        if layer + 1 == n_a:
            nexts.append((kv_norm_g, 0, None, None))
        outs = mm(f, "w_ff2", layer, out_dtype=F32, epilogue="resid", resid=(x, gt2, g_post_ffn, layer),
                  nexts=tuple(nexts), groups=groups)
        x = outs[0]
        if layer + 1 < depth:
            h = outs[1]
        if layer + 1 == n_a:
            hn = outs[-1]
    return x, jnp.stack(states, axis=0), k_new, v_new


def kernel(x_prompt, x_sample, c_prompt, c_sample, state_ret, cache_k, cache_v, page_table, w_ada, b_ada,
           g_pre_mix, g_post_mix, g_pre_ffn, g_post_ffn, w_ff1, w_ff2, a_w_in, a_w_out, a_gn_g, kv_norm_g,
           w_kv, b_w_q, b_w_out, b_sb):
    bp, seq, d = x_prompt.shape
    bs, dec_seq, _ = x_sample.shape
    depth = w_ada.shape[0]
    n_pool, page, _, _ = cache_k.shape
    n_pages = page_table.shape[1]
    past_len = n_pages * page
    hd = d // SB_HEADS
    dk, dv = d // RET_HEADS, 2 * d // RET_HEADS
    rs = SAMPLE_ROWS

    gains = (g_pre_mix.reshape(depth, 1, d), g_post_mix.reshape(depth, 1, d),
             g_pre_ffn.reshape(depth, 1, d), g_post_ffn.reshape(depth, 1, d), kv_norm_g.reshape(1, 1, d))
    wts = dict(w_ff1=w_ff1, w_ff2=w_ff2, a_w_in=a_w_in, a_w_out=a_w_out, w_kv=w_kv[None], b_w_q=b_w_q,
               b_w_out=b_w_out)
    cast = {}
    gn_g3 = a_gn_g.reshape(a_gn_g.shape[0], 1, 2 * d)
    log_g = jnp.log1p(-jnp.exp2(-5.0 - jnp.arange(RET_HEADS, dtype=F32)))

    n_c = bp + bs
    c_rows = -(-n_c // SUBLANES) * SUBLANES
    c_all = jnp.concatenate([c_prompt, c_sample, jnp.zeros((c_rows - n_c, d), F32)], axis=0)
    mods_all = _ada(c_all, w_ada, b_ada)
    mods_p = [[m[:, None, :] for m in jnp.split(mods_all[l, :bp], 6, axis=-1)] for l in range(depth)]
    mods_s = [[jnp.repeat(m, rs, axis=0)[None] for m in jnp.split(mods_all[l, bp:n_c], 6, axis=-1)]
              for l in range(depth)]

    x_s =jnp.pad(x_sample, ((0, 0), (0, rs - dec_seq), (0, 0))).reshape(bs * rs, d)
    cos_s, sin_s = _rope_tables(past_len + jnp.arange(rs, dtype=jnp.int32), dk // 2)

    def ret_sample(layer, proj):
        return _retention(proj, cos_s, sin_s, state_ret[layer], gn_g3[layer:layer + 1], log_g, batch=bs,
                          chunk=rs, valid=dec_seq, n_sub=1)

    def attn_sample(j, q, k_new, v_new):
        pad = ((0, 0), (0, (page - rs) * SB_HEADS), (0, 0))
        kn = jnp.pad(k_new.reshape(bs, rs * SB_HEADS, hd), pad)
        vn = jnp.pad(v_new.reshape(bs, rs * SB_HEADS, hd), pad)
        o = _sb_sample(q.astype(F32).reshape(bs, rs, d), kn, vn, cache_k, cache_v, page_table, b_sb[j],
                       n_q=dec_seq)
        return o.reshape(bs * rs, d).astype(BF16)

    y_s, st_s, k_s, v_s = _trunk(x_s, mods_s, 1, gains, wts, cast, ret_sample, attn_sample, bs * rs, F32,
                                 False)

    cos_p, sin_p = _rope_tables(jnp.arange(seq, dtype=jnp.int32), dk // 2)
    s0_p = jnp.zeros((bp, RET_HEADS, dk, dv), F32)

    def ret_prompt(layer, proj):
        return _retention(proj, cos_p, sin_p, s0_p, gn_g3[layer:layer + 1], log_g, batch=bp,
                          chunk=RET_CHUNK, valid=RET_CHUNK, n_sub=min(8, seq // RET_CHUNK))

    def attn_prompt(j, q, k_new, v_new):
        return _sb_prompt(q, k_new, v_new, b_sb[j], batch=bp)

    y_p, st_p, k_p, v_p = _trunk(x_prompt.reshape(bp * seq, d), mods_p, bp, gains, wts, cast, ret_prompt,
                                 attn_prompt, 1024, BF16, True)

    def unpad(a, *tail):
        return a.reshape(bs, rs, *tail)[:, :dec_seq]

    return (y_p.reshape(bp, seq, d), unpad(y_s, d), st_p, st_s,
            k_p.reshape(bp, seq, SB_HEADS, hd), v_p.reshape(bp, seq, SB_HEADS, hd),
            unpad(k_s, SB_HEADS, hd), unpad(v_s, SB_HEADS, hd))
```
